```python
import math, functools
import jax, jax.numpy as jnp
from jax import lax
import numpy as np

D_MODEL = 1024
BATCH = 8
SEQ = 2048
DEPTH = 4

GRID_W = 64
CTX_LEN = 256

NA_HEADS = 8
NA_HEAD_DIM = 64
NA_WIN_H = 8
NA_WIN_W = 16
DN_HEADS = 4
DN_HEAD_DIM = 128
DN_CONV = 5
DN_CHUNK = 64
SWA_Q_HEADS = 16
SWA_KV_HEADS = 2
SWA_HEAD_DIM = 64
SWA_WINDOW = 128
SWA_BLOCK = 128
ROPE_THETA = 10000.0
N_EXPERTS = 32
TOP_K = 4
D_EXPERT = 1024
SWIGLU_LIMIT = 7.0
SWIGLU_ALPHA = 1.702

LN_EPS = 1e-5
RMS_EPS = 1e-6
NEG_INF = -1e30
DEEPNORM_ALPHA = (2 * DEPTH) ** 0.25
DEEPNORM_BETA = (8 * DEPTH) ** -0.25

N_EVEN = (DEPTH + 1) // 2
N_ODD = DEPTH // 2
NA_WIDTH = NA_HEADS * NA_HEAD_DIM
DN_WIDTH = DN_HEADS * DN_HEAD_DIM
AB_SPLITS = (NA_WIDTH, 2 * NA_WIDTH, 3 * NA_WIDTH, 3 * NA_WIDTH + 3 * DN_WIDTH, 3 * NA_WIDTH + 4 * DN_WIDTH)
AB_IN = 3 * NA_WIDTH + 4 * DN_WIDTH + 4 * DN_HEADS
SWA_Q_WIDTH = SWA_Q_HEADS * SWA_HEAD_DIM
SWA_KV_WIDTH = SWA_KV_HEADS * SWA_HEAD_DIM
C_IN = SWA_Q_WIDTH + 2 * SWA_KV_WIDTH
F32 = jnp.float32

kernel_name = "hybrid_na_deltanet_swa_moe_diffusion_trunk"


def layer_norm(x, g, b):
    xf = x.astype(F32)
    mu = jnp.mean(xf, -1, keepdims=True)
    var = jnp.mean(jnp.square(xf - mu), -1, keepdims=True)
    return ((xf - mu) * lax.rsqrt(var + LN_EPS)).astype(x.dtype) * g + b


def softmax_parts(*logits):
    m = functools.reduce(jnp.maximum, [jnp.max(l, axis=-1, keepdims=True) for l in logits])
    e = [jnp.exp(l - m) for l in logits]
    denom = functools.reduce(jnp.add, [jnp.sum(t, axis=-1, keepdims=True) for t in e])
    return e, denom


def axial_rope_tables(n_tokens, head_dim):
    t = jnp.arange(n_tokens)
    n_freq = head_dim // 4
    inv = ROPE_THETA ** (-jnp.arange(n_freq, dtype=F32) / n_freq)
    row = (t // GRID_W).astype(F32)[:, None]
    col = (t % GRID_W).astype(F32)[:, None]
    ang = jnp.concatenate([row * inv, col * inv], -1)
    return jnp.cos(ang), jnp.sin(ang)


def apply_axial_rope(x, cos, sin):
    half = x.shape[-1] // 2
    x1, x2 = x[..., :half], x[..., half:]
    c = cos[:, None, :].astype(x.dtype)
    s = sin[:, None, :].astype(x.dtype)
    return jnp.concatenate([x1 * c - x2 * s, x2 * c + x1 * s], -1)


def context_attention(qc, kc, vc, sink=None):
    B, L, HQ, dh = qc.shape
    HKV = kc.shape[2]
    G = HQ // HKV
    qg = qc.reshape(B, L, HKV, G, dh) * dh ** -0.5
    s = jnp.einsum('bqkgd,blkd->bkgql', qg, kc, preferred_element_type=F32)
    if sink is None:
        parts = (s,)
    else:
        parts = (s, jnp.broadcast_to(sink.astype(F32).reshape(1, HKV, G, 1, 1), s.shape[:-1] + (1,)))
    e, denom = softmax_parts(*parts)
    o = jnp.einsum('bkgql,blkd->bkgqd', e[0], vc.astype(F32)) / denom
    return jnp.transpose(o, (0, 3, 1, 2, 4)).reshape(B, L, HQ * dh).astype(qc.dtype)


def neighbourhood_attention(q, k, v, kc, vc, rpb):
    B, N, H, dh = q.shape
    rows = N // GRID_W
    kh = min(NA_WIN_H, rows)
    r = jnp.arange(rows)
    key_rows = jnp.clip(r - kh // 2, 0, rows - kh)[:, None] + jnp.arange(kh)[None, :]
    col = jnp.arange(GRID_W)
    col_start = jnp.clip(col - NA_WIN_W // 2, 0, GRID_W - NA_WIN_W)
    col_in = (col[None, :] >= col_start[:, None]) & (col[None, :] < col_start[:, None] + NA_WIN_W)
    dy = key_rows - r[:, None] + NA_WIN_H - 1
    dx = jnp.clip(col[None, :] - col[:, None], 1 - NA_WIN_W, NA_WIN_W - 1) + NA_WIN_W - 1
    bias = rpb.astype(F32)[:, dy[:, None, :, None], dx[None, :, None, :]]
    qg = q.reshape(B, rows, GRID_W, H, dh) * dh ** -0.5
    kg = k.reshape(B, rows, GRID_W, H, dh)[:, key_rows]
    vg = v.reshape(B, rows, GRID_W, H, dh)[:, key_rows].reshape(B, rows, kh * GRID_W, H, dh)
    s_lat = jnp.einsum('brqhd,brjkhd->bhrqjk', qg, kg, preferred_element_type=F32) + bias
    s_lat = jnp.where(col_in[:, None, :], s_lat, NEG_INF).reshape(B, H, rows, GRID_W, kh * GRID_W)
    s_ctx = jnp.einsum('brqhd,blhd->bhrql', qg, kc, preferred_element_type=F32)
    (e_lat, e_ctx), denom = softmax_parts(s_lat, s_ctx)
    o = (jnp.einsum('bhrqj,brjhd->bhrqd', e_lat, vg.astype(F32))
         + jnp.einsum('bhrql,blhd->bhrqd', e_ctx, vc.astype(F32))) / denom
    return jnp.transpose(o, (0, 2, 3, 1, 4)).reshape(B, N, H * dh).astype(q.dtype)


def window_attention(q, k, v, kc, vc, sink):
    B, N, HQ, dh = q.shape
    HKV = k.shape[2]
    G = HQ // HKV
    P = SWA_BLOCK
    nb = N // P
    pad = ((0, 0), (P, P), (0, 0), (0, 0))
    band = jnp.arange(nb)[:, None] * P + jnp.arange(3 * P)[None, :]
    kb = jnp.pad(k, pad)[:, band]
    vb = jnp.pad(v, pad)[:, band]
    qpos = jnp.arange(nb)[:, None] * P + jnp.arange(P)[None, :]
    kpos = (band - P)[:, None, :]
    valid = (jnp.abs(qpos[:, :, None] - kpos) <= SWA_WINDOW) & (kpos >= 0) & (kpos < N)
    qb = q.reshape(B, nb, P, HKV, G, dh) * dh ** -0.5
    s_lat = jnp.einsum('bnqkgd,bnjkd->bkgnqj', qb, kb, preferred_element_type=F32)
    s_lat = jnp.where(valid, s_lat, NEG_INF)
    s_ctx = jnp.einsum('bnqkgd,blkd->bkgnql', qb, kc, preferred_element_type=F32)
    s_sink = jnp.broadcast_to(sink.astype(F32).reshape(1, HKV, G, 1, 1, 1), s_ctx.shape[:-1] + (1,))
    (e_lat, e_ctx, _), denom = softmax_parts(s_lat, s_ctx, s_sink)
    o = (jnp.einsum('bkgnqj,bnjkd->bkgnqd', e_lat, vb.astype(F32))
         + jnp.einsum('bkgnql,blkd->bkgnqd', e_ctx, vc.astype(F32))) / denom
    return jnp.transpose(o, (0, 3, 4, 1, 2, 5)).reshape(B, N, HQ * dh).astype(q.dtype)


def centred_depthwise_conv(x, w):
    K, C = w.shape
    return lax.conv_general_dilated(x, w[:, None, :].astype(x.dtype), window_strides=(1,),
                                    padding=[(K // 2, K // 2)], dimension_numbers=('NWC', 'WIO', 'NWC'),
                                    feature_group_count=C)


def l2norm(t):
    return t * lax.rsqrt(jnp.sum(t * t, -1, keepdims=True) + RMS_EPS)


def delta_inputs(qkv, ab, conv_w, a_log, dt_bias):
    B, T, _ = qkv.shape
    qkv = jax.nn.silu(centred_depthwise_conv(qkv, conv_w)).astype(F32)
    q, k, v = jnp.split(qkv, 3, axis=-1)
    to_heads = lambda t: jnp.swapaxes(t.reshape(B, T, DN_HEADS, DN_HEAD_DIM), 1, 2)
    q, k, v = to_heads(q), to_heads(k), to_heads(v)
    q = l2norm(q) * DN_HEAD_DIM ** -0.5
    k = l2norm(k)
    ab = ab.astype(F32).reshape(B, T, 2, 2, DN_HEADS)
    g = -jnp.exp(a_log.astype(F32)) * jax.nn.softplus(ab[:, :, 0] + dt_bias.astype(F32))
    beta = jax.nn.sigmoid(ab[:, :, 1])
    return q, k, v, jnp.transpose(g, (2, 0, 3, 1)), jnp.transpose(beta, (2, 0, 3, 1))


def gated_delta_chunked(q, k, v, g, beta, s0):
    B, H, T, dk = k.shape
    dv = v.shape[-1]
    C = DN_CHUNK
    nc = T // C
    q = q.reshape(B, H, nc, C, dk)
    k = k.reshape(B, H, nc, C, dk)
    v = v.reshape(B, H, nc, C, dv)
    g = jnp.cumsum(g.reshape(B, H, nc, C), -1)
    beta = beta.reshape(B, H, nc, C)
    kb = k * beta[..., None]
    vb = v * beta[..., None]
    incl = jnp.tril(jnp.ones((C, C), bool))
    strict = jnp.tril(jnp.ones((C, C), bool), -1)
    decay = jnp.where(incl, jnp.exp(jnp.where(incl, g[..., :, None] - g[..., None, :], 0.0)), 0.0)
    m = jnp.where(strict, jnp.einsum('bhnid,bhnjd->bhnij', kb, k) * decay, 0.0)
    eye = jnp.eye(C, dtype=F32)
    tmat = lax.linalg.triangular_solve(eye + m, jnp.broadcast_to(eye, m.shape), left_side=True,
                                       lower=True, unit_diagonal=True)
    u = jnp.einsum('bhnij,bhnjd->bhnid', tmat, vb)
    w = jnp.einsum('bhnij,bhnjd->bhnid', tmat, kb * jnp.exp(g)[..., None])
    a_intra = jnp.where(incl, jnp.einsum('bhnid,bhnjd->bhnij', q, k) * decay, 0.0)

    def step(s, xs):
        q_i, k_i, u_i, w_i, g_i, a_i = xs
        v_new = u_i - jnp.einsum('bhcd,bhde->bhce', w_i, s)
        o_i = jnp.einsum('bhcd,bhde->bhce', q_i * jnp.exp(g_i)[..., None], s) + jnp.einsum('bhij,bhje->bhie', a_i, v_new)
        g_last = g_i[..., -1:]
        s = s * jnp.exp(g_last)[..., None] + jnp.einsum('bhcd,bhce->bhde', k_i * jnp.exp(g_last - g_i)[..., None], v_new)
        return s, o_i

    xs = tuple(jnp.moveaxis(t, 2, 0) for t in (q, k, u, w, g, a_intra))
    s_final, o = lax.scan(step, s0, xs)
    return jnp.moveaxis(o, 0, 2).reshape(B, H, T, dv), s_final


def delta_bidirectional(q, k, v, g, beta, s0):
    o_f, s_f = gated_delta_chunked(q, k, v, g[0], beta[0], s0[0])
    flip = lambda t: jnp.flip(t, axis=2)
    o_b, s_b = gated_delta_chunked(flip(q), flip(k), flip(v), flip(g[1]), flip(beta[1]), s0[1])
    return o_f + flip(o_b), jnp.stack([s_f, s_b])


def gated_rmsnorm(o, z, w):
    B, H, T, dv = o.shape
    o = jnp.swapaxes(o, 1, 2)
    o = o * lax.rsqrt(jnp.mean(o * o, -1, keepdims=True) + RMS_EPS) * w.astype(F32)
    return (o * jax.nn.silu(z.astype(F32).reshape(B, T, H, dv))).reshape(B, T, H * dv)


def mixer_ab(h, hc, w_in, rpb, conv_w, a_log, dt_bias, norm_w, w_out, need_ctx):
    qa, ka, va, qkv_b, z, ab = jnp.split(h @ w_in, AB_SPLITS, axis=-1)
    qac, kac, vac, qkv_bc, zc, abc = jnp.split(hc @ w_in, AB_SPLITS, axis=-1)
    na = lambda t: t.reshape(t.shape[0], t.shape[1], NA_HEADS, NA_HEAD_DIM)
    o_a = neighbourhood_attention(na(qa), na(ka), na(va), na(kac), na(vac), rpb)
    B = h.shape[0]
    s0 = jnp.zeros((2, B, DN_HEADS, DN_HEAD_DIM, DN_HEAD_DIM), F32)
    o_bc, s_ctx = delta_bidirectional(*delta_inputs(qkv_bc, abc, conv_w, a_log, dt_bias), s0)
    o_b, _ = delta_bidirectional(*delta_inputs(qkv_b, ab, conv_w, a_log, dt_bias), s_ctx)
    o_b = gated_rmsnorm(o_b, z, norm_w).astype(h.dtype)
    out = jnp.concatenate([o_a, o_b], -1) @ w_out
    if not need_ctx:
        return out, None
    o_ac = context_attention(na(qac), na(kac), na(vac))
    o_bc = gated_rmsnorm(o_bc, zc, norm_w).astype(h.dtype)
    return out, jnp.concatenate([o_ac, o_bc], -1) @ w_out


def mixer_c(h, hc, w_in, sink, w_out, cos, sin, need_ctx):
    B, N, _ = h.shape
    L = hc.shape[1]
    p = h @ w_in
    q = apply_axial_rope(p[..., :SWA_Q_WIDTH].reshape(B, N, SWA_Q_HEADS, SWA_HEAD_DIM), cos, sin)
    k = apply_axial_rope(p[..., SWA_Q_WIDTH:SWA_Q_WIDTH + SWA_KV_WIDTH].reshape(B, N, SWA_KV_HEADS, SWA_HEAD_DIM), cos, sin)
    v = p[..., SWA_Q_WIDTH + SWA_KV_WIDTH:].reshape(B, N, SWA_KV_HEADS, SWA_HEAD_DIM)
    kvc = hc @ w_in[:, SWA_Q_WIDTH:]
    kc = kvc[..., :SWA_KV_WIDTH].reshape(B, L, SWA_KV_HEADS, SWA_HEAD_DIM)
    vc = kvc[..., SWA_KV_WIDTH:].reshape(B, L, SWA_KV_HEADS, SWA_HEAD_DIM)
    out = window_attention(q, k, v, kc, vc, sink) @ w_out
    if not need_ctx:
        return out, None
    qc = (hc @ w_in[:, :SWA_Q_WIDTH]).reshape(B, L, SWA_Q_HEADS, SWA_HEAD_DIM)
    return out, context_attention(qc, kc, vc, sink) @ w_out


def moe_ffn(t, w_router, b_router, w_gu, b_gu, w_down, b_down):
    logits = jnp.matmul(t, w_router, preferred_element_type=F32) + b_router.astype(F32)
    top_v, top_i = lax.top_k(logits, TOP_K)
    gates = jax.nn.softmax(top_v, axis=-1)
    combine = jnp.einsum('tk,tke->te', gates, jax.nn.one_hot(top_i, N_EXPERTS, dtype=F32))
    y = jnp.zeros(t.shape, F32)
    for e in range(N_EXPERTS):
        gu = t @ w_gu[e] + b_gu[e]
        gate = jnp.minimum(gu[:, :D_EXPERT], SWIGLU_LIMIT)
        up = jnp.clip(gu[:, D_EXPERT:], -SWIGLU_LIMIT, SWIGLU_LIMIT)
        hidden = gate * jax.nn.sigmoid(SWIGLU_ALPHA * gate) * (up + 1)
        y = y + combine[:, e:e + 1] * (hidden @ w_down[e] + b_down[e]).astype(F32)
    return y.astype(t.dtype)


def setup_inputs(seed: int = 0) -> dict:
    key = jax.random.key(seed)
    ks = jax.random.split(key, 24)
    nrm = lambda k, shape, s: jax.random.normal(k, shape, F32) * s
    D = D_MODEL
    dt = jnp.exp(jax.random.uniform(ks[12], (N_EVEN, 2, DN_HEADS), F32, math.log(1e-3), math.log(1e-1)))
    return {
        "x": nrm(ks[0], (BATCH, SEQ, D), 1.0),
        "c": nrm(ks[1], (BATCH, D), 1.0),
        "ctx": nrm(ks[2], (BATCH, CTX_LEN, D), 1.0),
        "c_ctx": nrm(ks[3], (D,), 1.0),
        "w_mod": nrm(ks[4], (DEPTH, D, 6 * D), 0.5 * D ** -0.5),
        "b_mod": nrm(ks[5], (DEPTH, 6 * D), 0.02),
        "ln_g": 1.0 + nrm(ks[6], (DEPTH, 2, D), 0.05),
        "ln_b": nrm(ks[7], (DEPTH, 2, D), 0.02),
        "w_in_ab": nrm(ks[8], (N_EVEN, D, AB_IN), D ** -0.5),
        "na_rpb": nrm(ks[9], (N_EVEN, NA_HEADS, 2 * NA_WIN_H - 1, 2 * NA_WIN_W - 1), 0.5),
        "dn_conv": nrm(ks[10], (N_EVEN, DN_CONV, 3 * DN_WIDTH), DN_CONV ** -0.5),
        "dn_a_log": jnp.log(jax.random.uniform(ks[11], (N_EVEN, 2, DN_HEADS), F32, 1.0, 16.0)),
        "dn_dt_bias": dt + jnp.log(-jnp.expm1(-dt)),
        "dn_norm_w": 1.0 + nrm(ks[13], (N_EVEN, DN_HEAD_DIM), 0.05),
        "w_out_ab": nrm(ks[14], (N_EVEN, NA_WIDTH + DN_WIDTH, D), DEEPNORM_BETA * (NA_WIDTH + DN_WIDTH) ** -0.5),
        "w_in_c": nrm(ks[15], (N_ODD, D, C_IN), D ** -0.5),
        "swa_sink": nrm(ks[16], (N_ODD, SWA_Q_HEADS), 1.0),
        "w_out_c": nrm(ks[17], (N_ODD, SWA_Q_WIDTH, D), DEEPNORM_BETA * SWA_Q_WIDTH ** -0.5),
        "w_router": nrm(ks[18], (DEPTH, D, N_EXPERTS), D ** -0.5),
        "b_router": nrm(ks[19], (DEPTH, N_EXPERTS), 0.01),
        "w_gu": nrm(ks[20], (DEPTH, N_EXPERTS, D, 2 * D_EXPERT), D ** -0.5),
        "b_gu": nrm(ks[21], (DEPTH, N_EXPERTS, 2 * D_EXPERT), 0.02),
        "w_down": nrm(ks[22], (DEPTH, N_EXPERTS, D_EXPERT, D), DEEPNORM_BETA * D_EXPERT ** -0.5),
        "b_down": nrm(ks[23], (DEPTH, N_EXPERTS, D), 0.02),
    }


def reference(x, c, ctx, c_ctx, w_mod, b_mod, ln_g, ln_b, w_in_ab, na_rpb, dn_conv, dn_a_log, dn_dt_bias,
              dn_norm_w, w_out_ab, w_in_c, swa_sink, w_out_c, w_router, b_router, w_gu, b_gu, w_down, b_down):
    B, N, D = x.shape
    L = ctx.shape[1]
    cos, sin = axial_rope_tables(N, SWA_HEAD_DIM)
    silu_c = jax.nn.silu(c)
    silu_cc = jax.nn.silu(c_ctx)
    h_lat, h_ctx = x, ctx
    for layer in range(DEPTH):
        last = layer == DEPTH - 1
        mod = (silu_c @ w_mod[layer] + b_mod[layer])[:, None, :]
        modc = silu_cc @ w_mod[layer] + b_mod[layer]
        sh1, sc1, g1, sh2, sc2, g2 = jnp.split(mod, 6, axis=-1)
        csh1, csc1, cg1, csh2, csc2, cg2 = jnp.split(modc, 6, axis=-1)
        a_lat = h_lat * (1 + sc1) + sh1
        a_ctx = h_ctx * (1 + csc1) + csh1
        i = layer // 2
        if layer % 2 == 0:
            m_lat, m_ctx = mixer_ab(a_lat, a_ctx, w_in_ab[i], na_rpb[i], dn_conv[i], dn_a_log[i], dn_dt_bias[i],
                                    dn_norm_w[i], w_out_ab[i], not last)
        else:
            m_lat, m_ctx = mixer_c(a_lat, a_ctx, w_in_c[i], swa_sink[i], w_out_c[i], cos, sin, not last)
        h_lat = layer_norm(DEEPNORM_ALPHA * h_lat + g1 * m_lat, ln_g[layer, 0], ln_b[layer, 0])
        f_lat = (h_lat * (1 + sc2) + sh2).reshape(B * N, D)
        if last:
            f = moe_ffn(f_lat, w_router[layer], b_router[layer], w_gu[layer], b_gu[layer], w_down[layer], b_down[layer])
        else:
            h_ctx = layer_norm(DEEPNORM_ALPHA * h_ctx + cg1 * m_ctx, ln_g[layer, 0], ln_b[layer, 0])
            f_ctx = (h_ctx * (1 + csc2) + csh2).reshape(B * L, D)
            f = moe_ffn(jnp.concatenate([f_lat, f_ctx], 0), w_router[layer], b_router[layer], w_gu[layer],
                        b_gu[layer], w_down[layer], b_down[layer])
            h_ctx = layer_norm(DEEPNORM_ALPHA * h_ctx + cg2 * f[B * N:].reshape(B, L, D), ln_g[layer, 1], ln_b[layer, 1])
        h_lat = layer_norm(DEEPNORM_ALPHA * h_lat + g2 * f[:B * N].reshape(B, N, D), ln_g[layer, 1], ln_b[layer, 1])
    return h_lat
```

```python
import functools

import jax
import jax.numpy as jnp
import numpy as np
from jax import lax
from jax.experimental import pallas as pl
from jax.experimental.pallas import tpu as pltpu

F32 = jnp.float32
BF16 = jnp.bfloat16

D_MODEL = 1024
GRID_W = 64
NA_HEADS = 8
NA_HEAD_DIM = 64
NA_WIN_H = 8
NA_WIN_W = 16
DN_HEADS = 4
DN_HEAD_DIM = 128
DN_CONV = 5
DN_CHUNK = 64
SWA_Q_HEADS = 16
SWA_KV_HEADS = 2
SWA_HEAD_DIM = 64
SWA_WINDOW = 128
ROPE_THETA = 10000.0
N_EXPERTS = 32
TOP_K = 4
D_EXPERT = 1024
SWIGLU_LIMIT = 7.0
SWIGLU_ALPHA = 1.702
LN_EPS = 1e-5
RMS_EPS = 1e-6
NEG_INF = -1e30

LANES = 128
SUBLANES = 8
TOK_TILE = 256
MOE_TILE = 256
MOE_CHUNK_MAX = 3072
VMEM_LIMIT = 56 * 1024 * 1024

NA_WIDTH = NA_HEADS * NA_HEAD_DIM
DN_WIDTH = DN_HEADS * DN_HEAD_DIM
NA_QROWS = 4
NA_KROWS = 12


def _params(sem):
    return pltpu.CompilerParams(dimension_semantics=sem, vmem_limit_bytes=VMEM_LIMIT)


def _sigmoid(x):
    return 1.0 / (1.0 + jnp.exp(-x))


def _dot(a, b):
    return jnp.dot(a, b, preferred_element_type=F32)


def _dot_nt(a, b):
    return lax.dot_general(a, b, (((1,), (1,)), ((), ())), preferred_element_type=F32)


def _moe_layout_load(ref, rows):
    return jnp.concatenate([ref[pl.ds(s, rows, stride=SUBLANES), :] for s in range(SUBLANES)], axis=1)


def _moe_layout_store(ref, val, rows):
    for s in range(SUBLANES):
        ref[pl.ds(s, rows, stride=SUBLANES), :] = val[:, s * LANES:(s + 1) * LANES]


def _layer_norm(y, g, b):
    mu = jnp.mean(y, axis=-1, keepdims=True)
    d = y - mu
    var = jnp.mean(d * d, axis=-1, keepdims=True)
    return d * lax.rsqrt(var + LN_EPS) * g + b


def _mod_kernel(s_ref, w_ref, b_ref, o_ref):
    s = s_ref[...]
    s = s * _sigmoid(s)
    o_ref[0] = _dot(s.astype(BF16), w_ref[0].astype(BF16)) + b_ref[0]


def _modulation(cc, w_mod, b_mod):
    depth = w_mod.shape[0]
    rows = cc.shape[0]
    nblk = w_mod.shape[2] // D_MODEL
    return pl.pallas_call(
        _mod_kernel,
        grid=(depth, nblk),
        in_specs=[pl.BlockSpec((rows, D_MODEL), lambda l, j: (0, 0)),
                  pl.BlockSpec((1, D_MODEL, D_MODEL), lambda l, j: (l, 0, j)),
                  pl.BlockSpec((1, 1, D_MODEL), lambda l, j: (l, 0, j))],
        out_specs=pl.BlockSpec((1, rows, D_MODEL), lambda l, j: (l, 0, j)),
        out_shape=jax.ShapeDtypeStruct((depth, rows, nblk * D_MODEL), F32),
        compiler_params=_params(("arbitrary", "arbitrary")),
        name="modulation",
    )(cc, w_mod, b_mod.reshape(depth, 1, -1))


def _pin_kernel(*refs, first, n_rope, n_att, alpha):
    refs = list(refs)
    h_ref = refs.pop(0)
    if not first:
        moe_ref, g2_ref, lg_ref, lb_ref = refs[:4]
        refs = refs[4:]
    sc_ref, sh_ref, w_ref = refs[:3]
    refs = refs[3:]
    if n_rope:
        cos_ref, sin_ref = refs[:2]
        refs = refs[2:]
    if not first:
        hout_ref = refs.pop(0)
    att_ref = refs.pop(0)
    dn_ref = refs.pop(0) if refs else None

    h = h_ref[...]
    if not first:
        m = _moe_layout_load(moe_ref, TOK_TILE)
        h = _layer_norm(alpha * h + g2_ref[0] * m, lg_ref[...], lb_ref[...])
        hout_ref[...] = h
    a = h * (1.0 + sc_ref[0]) + sh_ref[0]
    acc = _dot(a.astype(BF16), w_ref[...])
    if n_rope:
        cos = cos_ref[...]
        sin = sin_ref[...]
        for g in range(n_att // LANES):
            blk = acc[:, g * LANES:(g + 1) * LANES]
            if g < n_rope:
                blk = blk * cos + pltpu.roll(blk, LANES // 2, axis=1) * sin
            att_ref[:, g * LANES:(g + 1) * LANES] = blk.astype(BF16)
    else:
        att_ref[...] = acc[:, :n_att].astype(BF16)
    if dn_ref is not None:
        dn_ref[...] = acc[:, n_att:]


def _mod_spec(k, n_lat_tiles, lat_tiles_per_batch, n_batch):
    def imap(i):
        return (jnp.where(i < n_lat_tiles, i // lat_tiles_per_batch, n_batch), 0, k)
    return pl.BlockSpec((1, 1, D_MODEL), imap)


def _proj_in(h, moe_prev, mod_prev, ln_prev, mod, w, geom, *, n_att, rope=None, alpha=1.0):
    T = h.shape[0]
    n_tiles = T // TOK_TILE
    n_lat_tiles, per_batch, n_batch = geom
    first = moe_prev is None
    ncols = w.shape[1]
    row = pl.BlockSpec((TOK_TILE, D_MODEL), lambda i: (i, 0))
    vec = pl.BlockSpec((1, D_MODEL), lambda i: (0, 0))
    ms = functools.partial(_mod_spec, n_lat_tiles=n_lat_tiles, lat_tiles_per_batch=per_batch, n_batch=n_batch)
    args, specs = [h], [row]
    if not first:
        args += [moe_prev, mod_prev, ln_prev[0:1], ln_prev[1:2]]
        specs += [pl.BlockSpec((TOK_TILE * SUBLANES, LANES), lambda i: (i, 0)), ms(5), vec, vec]
    args += [mod, mod, w]
    specs += [ms(1), ms(0), pl.BlockSpec((D_MODEL, ncols), lambda i: (0, 0))]
    n_rope = 0
    if rope is not None:
        cos_t, sin_t, n_rope = rope
        tab = pl.BlockSpec((TOK_TILE, LANES), lambda i: (jnp.where(i < n_lat_tiles, i % per_batch, per_batch), 0))
        args += [cos_t, sin_t]
        specs += [tab, tab]
    out_shapes, out_specs = [], []
    if not first:
        out_shapes.append(jax.ShapeDtypeStruct((T, D_MODEL), F32))
        out_specs.append(row)
    out_shapes.append(jax.ShapeDtypeStruct((T, n_att), BF16))
    out_specs.append(pl.BlockSpec((TOK_TILE, n_att), lambda i: (i, 0)))
    if ncols > n_att:
        out_shapes.append(jax.ShapeDtypeStruct((T, ncols - n_att), F32))
        out_specs.append(pl.BlockSpec((TOK_TILE, ncols - n_att), lambda i: (i, 0)))
    outs = pl.pallas_call(
        functools.partial(_pin_kernel, first=first, n_rope=n_rope, n_att=n_att, alpha=alpha),
        grid=(n_tiles,), in_specs=specs, out_specs=out_specs, out_shape=out_shapes,
        compiler_params=_params(("arbitrary",)), name="proj_in",
    )(*args)
    outs = list(outs)
    h_new = h if first else outs.pop(0)
    att = outs.pop(0)
    dn = outs.pop(0) if outs else None
    return h_new, att, dn


def _final_kernel(h_ref, moe_ref, g2_ref, lg_ref, lb_ref, o_ref, *, alpha):
    m = _moe_layout_load(moe_ref, TOK_TILE)
    o_ref[...] = _layer_norm(alpha * h_ref[...] + g2_ref[0] * m, lg_ref[...], lb_ref[...])


def _final_combine(h, moe, mod, ln, geom, *, alpha):
    T = h.shape[0]
    n_lat_tiles, per_batch, n_batch = geom
    row = pl.BlockSpec((TOK_TILE, D_MODEL), lambda i: (i, 0))
    vec = pl.BlockSpec((1, D_MODEL), lambda i: (0, 0))
    return pl.pallas_call(
        functools.partial(_final_kernel, alpha=alpha),
        grid=(T // TOK_TILE,),
        in_specs=[row, pl.BlockSpec((TOK_TILE * SUBLANES, LANES), lambda i: (i, 0)),
                  _mod_spec(5, n_lat_tiles, per_batch, n_batch), vec, vec],
        out_specs=row, out_shape=jax.ShapeDtypeStruct((T, D_MODEL), F32),
        compiler_params=_params(("arbitrary",)), name="final_combine",
    )(h, moe, mod, ln[0:1], ln[1:2])


def _pout_kernel(*refs, n_in, alpha):
    xs = refs[:n_in]
    ws = refs[n_in:2 * n_in]
    (h_ref, g1_ref, lg_ref, lb_ref, sc_ref, sh_ref, wr_ref, br_ref,
     hout_ref, f_ref, idx_ref, gate_ref) = refs[2 * n_in:]
    m = _dot(xs[0][...], ws[0][...])
    for x_ref, w_ref in zip(xs[1:], ws[1:]):
        m = m + _dot(x_ref[...], w_ref[...])
    h1 = _layer_norm(alpha * h_ref[...] + g1_ref[0] * m, lg_ref[...], lb_ref[...])
    hout_ref[...] = h1
    f = h1 * (1.0 + sc_ref[0]) + sh_ref[0]
    _moe_layout_store(f_ref, f, TOK_TILE)
    logits = _dot(f.astype(BF16), wr_ref[...]) + br_ref[...]
    lane = lax.broadcasted_iota(jnp.int32, logits.shape, 1)
    work = logits
    vals, idxs = [], []
    for _ in range(TOP_K):
        v = jnp.max(work, axis=1, keepdims=True)
        ix = jnp.min(jnp.where(work == v, lane, LANES), axis=1, keepdims=True)
        vals.append(v)
        idxs.append(ix)
        work = jnp.where(lane == ix, -jnp.inf, work)
    es = [jnp.exp(v - vals[0]) for v in vals]
    den = es[0]
    for e in es[1:]:
        den = den + e
    idx_out = jnp.zeros(logits.shape, jnp.int32)
    gate_out = jnp.zeros(logits.shape, F32)
    for k in range(TOP_K):
        idx_out = jnp.where(lane == k, idxs[k], idx_out)
        gate_out = jnp.where(lane == k, es[k] / den, gate_out)
    idx_ref[...] = idx_out
    gate_ref[...] = gate_out


def _proj_out(xs, ws, h, mod, ln, w_router, b_router, geom, n_rows, *, alpha):
    n_lat_tiles, per_batch, n_batch = geom
    row = pl.BlockSpec((TOK_TILE, D_MODEL), lambda i: (i, 0))
    vec = pl.BlockSpec((1, D_MODEL), lambda i: (0, 0))
    lane_row = pl.BlockSpec((TOK_TILE, LANES), lambda i: (i, 0))
    ms = functools.partial(_mod_spec, n_lat_tiles=n_lat_tiles, lat_tiles_per_batch=per_batch, n_batch=n_batch)
    specs = [pl.BlockSpec((TOK_TILE, x.shape[1]), lambda i: (i, 0)) for x in xs]
    specs += [pl.BlockSpec(w.shape, lambda i: (0, 0)) for w in ws]
    specs += [row, ms(2), vec, vec, ms(4), ms(3),
              pl.BlockSpec((D_MODEL, LANES), lambda i: (0, 0)), pl.BlockSpec((1, LANES), lambda i: (0, 0))]
    return pl.pallas_call(
        functools.partial(_pout_kernel, n_in=len(xs), alpha=alpha),
        grid=(n_rows // TOK_TILE,), in_specs=specs,
        out_specs=[row, pl.BlockSpec((TOK_TILE * SUBLANES, LANES), lambda i: (i, 0)), lane_row, lane_row],
        out_shape=[jax.ShapeDtypeStruct((n_rows, D_MODEL), F32),
                   jax.ShapeDtypeStruct((n_rows * SUBLANES, LANES), F32),
                   jax.ShapeDtypeStruct((n_rows, LANES), jnp.int32),
                   jax.ShapeDtypeStruct((n_rows, LANES), F32)],
        compiler_params=_params(("arbitrary",)), name="proj_out",
    )(*xs, *ws, h, mod, ln[0:1], ln[1:2], mod, mod, w_router, b_router)


def _split3(x):
    hi = x.astype(BF16)
    r = x - hi.astype(F32)
    mid = r.astype(BF16)
    lo = (r - mid.astype(F32)).astype(BF16)
    return hi, mid, lo


def _moe_kernel(tok_ref, toff_ref, tcnt_ref, rcnt_ref, f_ref, gw_ref, wgu_ref, bgu_ref, wd_ref, bd_ref,
                o_ref, xg_ref, *, n_tiles):
    c = pl.program_id(0)
    e = pl.program_id(1)

    @pl.when(e == 0)
    def _():
        o_ref[...] = jnp.zeros(o_ref.shape, F32)

    @pl.when((c == 0) & (e == 0))
    def _():
        xg_ref[...] = jnp.zeros(xg_ref.shape, F32)

    g = c * N_EXPERTS + e
    tile0 = toff_ref[g]
    n_rows = rcnt_ref[g]
    ii = lax.broadcasted_iota(jnp.int32, (MOE_TILE, MOE_TILE), 0)
    jj = lax.broadcasted_iota(jnp.int32, (MOE_TILE, MOE_TILE), 1)
    ones = jnp.ones((MOE_TILE, LANES), BF16)

    def tile(r, carry):
        tix = tile0 + r
        slot0 = (c * n_tiles + tix) * MOE_TILE
        n_valid = jnp.minimum(MOE_TILE, n_rows - r * MOE_TILE)

        def gather(j, _):
            t = tok_ref[slot0 + j]
            xg_ref[pl.ds(pl.multiple_of(j * SUBLANES, SUBLANES), SUBLANES), :] = (
                f_ref[pl.ds(pl.multiple_of(t * SUBLANES, SUBLANES), SUBLANES), :])
            return 0

        lax.fori_loop(0, n_valid, gather, 0)
        x = _moe_layout_load(xg_ref, MOE_TILE).astype(BF16)
        gu = _dot(x, wgu_ref[0]) + bgu_ref[0]
        gate = jnp.minimum(gu[:, :D_EXPERT], SWIGLU_LIMIT)
        up = jnp.clip(gu[:, D_EXPERT:], -SWIGLU_LIMIT, SWIGLU_LIMIT)
        hid = gate * _sigmoid(SWIGLU_ALPHA * gate) * (up + 1.0)
        y = _dot(hid.astype(BF16), wd_ref[0]) + bd_ref[0]
        gw_row = jnp.broadcast_to(gw_ref[pl.ds(tix, 1), :], (MOE_TILE, MOE_TILE))
        diag = jnp.where(ii == jj, gw_row, 0.0)
        gcol = sum(_dot(p, ones) for p in _split3(diag))
        for s in range(SUBLANES):
            xg_ref[pl.ds(s, MOE_TILE, stride=SUBLANES), :] = y[:, s * LANES:(s + 1) * LANES] * gcol

        def scatter(j, _):
            t = tok_ref[slot0 + j]
            rows = pl.ds(pl.multiple_of(t * SUBLANES, SUBLANES), SUBLANES)
            o_ref[rows, :] = o_ref[rows, :] + xg_ref[pl.ds(pl.multiple_of(j * SUBLANES, SUBLANES), SUBLANES), :]
            return 0

        lax.fori_loop(0, n_valid, scatter, 0)
        return carry

    lax.fori_loop(0, tcnt_ref[g], tile, 0)


def _moe_chunk(n_rows):
    c = MOE_CHUNK_MAX
    while n_rows % c:
        c -= TOK_TILE
    return c


def _moe_metadata(idx, gates, chunk):
    n_rows = idx.shape[0]
    n_chunks = n_rows // chunk
    n_assign = chunk * TOP_K
    n_tiles = n_assign // MOE_TILE + N_EXPERTS
    e = idx[:, :TOP_K].reshape(n_chunks, n_assign)
    gv = gates[:, :TOP_K].reshape(n_chunks, n_assign)
    onehot = (e[:, :, None] == jnp.arange(N_EXPERTS, dtype=jnp.int32)).astype(jnp.int32)
    rank = jnp.take_along_axis(jnp.cumsum(onehot, axis=1) - onehot, e[:, :, None], axis=2)[:, :, 0]
    cnt = jnp.sum(onehot, axis=1)
    tcnt = (cnt + MOE_TILE - 1) // MOE_TILE
    toff = jnp.cumsum(tcnt, axis=1) - tcnt
    slot = jnp.take_along_axis(toff, e, axis=1) * MOE_TILE + rank
    slot = slot + jnp.arange(n_chunks, dtype=jnp.int32)[:, None] * (n_tiles * MOE_TILE)
    tok_local = jnp.broadcast_to(jnp.arange(n_assign, dtype=jnp.int32) // TOP_K, (n_chunks, n_assign))
    tok = jnp.zeros((n_chunks * n_tiles * MOE_TILE,), jnp.int32).at[slot.reshape(-1)].set(tok_local.reshape(-1))
    gw = jnp.zeros((n_chunks * n_tiles * MOE_TILE,), F32).at[slot.reshape(-1)].set(gv.reshape(-1))
    return (tok, toff.reshape(-1).astype(jnp.int32), tcnt.reshape(-1).astype(jnp.int32),
            cnt.reshape(-1).astype(jnp.int32), gw.reshape(n_chunks * n_tiles, MOE_TILE), n_tiles)


def _moe(f_moe, idx, gates, wgu, bgu, wd, bd):
    n_rows = idx.shape[0]
    chunk = _moe_chunk(n_rows)
    tok, toff, tcnt, rcnt, gw, n_tiles = _moe_metadata(idx, gates, chunk)
    resident = pl.BlockSpec((chunk * SUBLANES, LANES), lambda c, e, *_: (c, 0), pipeline_mode=pl.Buffered(1))
    grid_spec = pltpu.PrefetchScalarGridSpec(
        num_scalar_prefetch=4,
        grid=(n_rows // chunk, N_EXPERTS),
        in_specs=[resident,
                  pl.BlockSpec((n_tiles, MOE_TILE), lambda c, e, *_: (c, 0)),
                  pl.BlockSpec((1, D_MODEL, 2 * D_EXPERT), lambda c, e, *_: (e, 0, 0)),
                  pl.BlockSpec((1, 1, 2 * D_EXPERT), lambda c, e, *_: (e, 0, 0)),
                  pl.BlockSpec((1, D_EXPERT, D_MODEL), lambda c, e, *_: (e, 0, 0)),
                  pl.BlockSpec((1, 1, D_MODEL), lambda c, e, *_: (e, 0, 0))],
        out_specs=pl.BlockSpec((chunk * SUBLANES, LANES), lambda c, e, *_: (c, 0), pipeline_mode=pl.Buffered(1)),
        scratch_shapes=[pltpu.VMEM((MOE_TILE * SUBLANES, LANES), F32)],
    )
    return pl.pallas_call(
        functools.partial(_moe_kernel, n_tiles=n_tiles),
        grid_spec=grid_spec,
        out_shape=jax.ShapeDtypeStruct((n_rows * SUBLANES, LANES), F32),
        compiler_params=_params(("arbitrary", "arbitrary")), name="moe",
    )(tok, toff, tcnt, rcnt, f_moe, gw, wgu, bgu.reshape(N_EXPERTS, 1, -1), wd, bd.reshape(N_EXPERTS, 1, -1))


def _softmax_pv(s_lat, s_ctx, v_lat, v_ctx, extra=None):
    m = jnp.maximum(jnp.max(s_lat, axis=1, keepdims=True), jnp.max(s_ctx, axis=1, keepdims=True))
    if extra is not None:
        m = jnp.maximum(m, extra)
    e_lat = jnp.exp(s_lat - m)
    e_ctx = jnp.exp(s_ctx - m)
    den = jnp.sum(e_lat, axis=1, keepdims=True) + jnp.sum(e_ctx, axis=1, keepdims=True)
    if extra is not None:
        den = den + jnp.exp(extra - m)
    o = _dot(e_lat.astype(BF16), v_lat) + _dot(e_ctx.astype(BF16), v_ctx)
    return o / den


def _na_kernel(q_ref, k_ref, v_ref, kc_ref, vc_ref, bias_ref, o_ref, *, n_blocks, n_keys):
    blk = pl.program_id(2)
    key_row0 = jnp.clip(NA_QROWS * blk - NA_WIN_H // 2, 0, n_blocks * NA_QROWS - NA_KROWS)
    start = pl.multiple_of(key_row0 * GRID_W, GRID_W)
    q = q_ref[...]
    kl = k_ref[pl.ds(start, n_keys), :]
    vl = v_ref[pl.ds(start, n_keys), :]
    kc = kc_ref[...]
    vc = vc_ref[...]
    lane = lax.broadcasted_iota(jnp.int32, (1, LANES), 1)
    scale = NA_HEAD_DIM ** -0.5
    outs = []
    for a in range(2):
        sel = (lane < NA_HEAD_DIM) if a == 0 else (lane >= NA_HEAD_DIM)
        qa = jnp.where(sel, q, jnp.zeros_like(q)) * scale
        s_lat = _dot_nt(qa, kl) + bias_ref[0, a]
        s_ctx = _dot_nt(qa, kc)
        outs.append(_softmax_pv(s_lat, s_ctx, vl, vc))
    o_ref[...] = jnp.where(lane < NA_HEAD_DIM, outs[0], outs[1]).astype(BF16)


def _na_bias_table(rpb, rows):
    H = rpb.shape[0]
    qc = np.arange(GRID_W)
    col_start = np.clip(qc - NA_WIN_W // 2, 0, GRID_W - NA_WIN_W)
    col_in = (qc[None, :] >= col_start[:, None]) & (qc[None, :] < col_start[:, None] + NA_WIN_W)
    dx = np.clip(qc[None, :] - qc[:, None], 1 - NA_WIN_W, NA_WIN_W - 1) + NA_WIN_W - 1
    tabs = []
    n_blocks = rows // NA_QROWS
    for blk in (0, 1, n_blocks - 1):
        r = NA_QROWS * blk + np.arange(NA_QROWS)
        key_row0 = int(np.clip(NA_QROWS * blk - NA_WIN_H // 2, 0, rows - NA_KROWS))
        kr = key_row0 + np.arange(NA_KROWS)
        win0 = np.clip(r - NA_WIN_H // 2, 0, rows - NA_WIN_H)
        row_in = (kr[None, :] >= win0[:, None]) & (kr[None, :] < win0[:, None] + NA_WIN_H)
        dy = np.clip(kr[None, :] - r[:, None] + NA_WIN_H - 1, 0, 2 * NA_WIN_H - 2)
        bias = rpb.astype(F32)[:, dy[:, None, :, None], dx[None, :, None, :]]
        mask = row_in[:, None, :, None] & col_in[None, :, None, :]
        tabs.append(jnp.where(mask[None], bias, NEG_INF).reshape(H, NA_QROWS * GRID_W, NA_KROWS * GRID_W))
    tabs.append(jnp.full_like(tabs[0], NEG_INF))
    return jnp.stack(tabs)


def _neighbourhood_attention(att, bias_tab, n_batch, n_lat, n_ctx):
    T = att.shape[0]
    qrows = NA_QROWS * GRID_W
    n_blocks = n_lat // qrows
    n_keys = NA_KROWS * GRID_W
    pairs = NA_WIDTH // LANES
    assert n_ctx == qrows
    lat_blocks = n_batch * n_blocks

    def q_map(b, hp, blk):
        return (jnp.where(blk < n_blocks, b * n_blocks + blk, lat_blocks + b), hp)

    def cls_map(b, hp, blk):
        cls = jnp.where(blk == 0, 0, jnp.where(blk == n_blocks - 1, 2, jnp.where(blk == n_blocks, 3, 1)))
        return (cls, hp, 0, 0)

    ctx_row = n_batch * n_lat // n_ctx
    return pl.pallas_call(
        functools.partial(_na_kernel, n_blocks=n_blocks, n_keys=n_keys),
        grid=(n_batch, pairs, n_blocks + 1),
        in_specs=[pl.BlockSpec((qrows, LANES), q_map),
                  pl.BlockSpec((n_lat, LANES), lambda b, hp, blk: (b, pairs + hp)),
                  pl.BlockSpec((n_lat, LANES), lambda b, hp, blk: (b, 2 * pairs + hp)),
                  pl.BlockSpec((n_ctx, LANES), lambda b, hp, blk: (ctx_row + b, pairs + hp)),
                  pl.BlockSpec((n_ctx, LANES), lambda b, hp, blk: (ctx_row + b, 2 * pairs + hp)),
                  pl.BlockSpec((1, 2, qrows, n_keys), cls_map)],
        out_specs=pl.BlockSpec((qrows, LANES), q_map),
        out_shape=jax.ShapeDtypeStruct((T, NA_WIDTH), BF16),
        compiler_params=_params(("arbitrary", "arbitrary", "arbitrary")), name="na_attention",
    )(att, att, att, att, att, bias_tab)


SWA_BLOCK = 128
SWA_BAND = 3 * SWA_BLOCK


def _swa_kernel(sink_ref, q_ref, k_ref, v_ref, kc_ref, vc_ref, o_ref, *, n_lat, group):
    kvh = pl.program_id(1)
    nb = pl.program_id(2)
    n_lat_blocks = n_lat // SWA_BLOCK
    start = pl.multiple_of(jnp.clip((nb - 1) * SWA_BLOCK, 0, n_lat - SWA_BAND), SWA_BLOCK)
    kl = k_ref[pl.ds(start, SWA_BAND), :]
    vl = v_ref[pl.ds(start, SWA_BAND), :]
    q = q_ref[...]
    lane = lax.broadcasted_iota(jnp.int32, (1, LANES), 1)
    first_head = (lane & (SWA_HEAD_DIM - 1)) < SWA_HEAD_DIM // 2
    scale = SWA_HEAD_DIM ** -0.5
    parts, sinks = [], []
    for g in range(group):
        qp = q[:, (g // 2) * LANES:(g // 2 + 1) * LANES]
        sel = first_head if g % 2 == 0 else jnp.logical_not(first_head)
        parts.append(jnp.where(sel, qp, jnp.zeros_like(qp)) * scale)
        sinks.append(jnp.full((SWA_BLOCK, 1), sink_ref[kvh * group + g], F32))
    qs = jnp.concatenate(parts, axis=0)
    sink = jnp.concatenate(sinks, axis=0)
    s_lat = _dot_nt(qs, kl)
    rows = group * SWA_BLOCK
    qpos = nb * SWA_BLOCK + (lax.broadcasted_iota(jnp.int32, (rows, 1), 0) & (SWA_BLOCK - 1))
    kpos = start + lax.broadcasted_iota(jnp.int32, (1, SWA_BAND), 1)
    valid = (jnp.abs(qpos - kpos) <= SWA_WINDOW) & (nb < n_lat_blocks)
    s_lat = jnp.where(valid, s_lat, NEG_INF)
    s_ctx = _dot_nt(qs, kc_ref[...])
    o = _softmax_pv(s_lat, s_ctx, vl, vc_ref[...], extra=sink)
    for p in range(group // 2):
        o_a = o[(2 * p) * SWA_BLOCK:(2 * p + 1) * SWA_BLOCK]
        o_b = o[(2 * p + 1) * SWA_BLOCK:(2 * p + 2) * SWA_BLOCK]
        o_ref[:, p * LANES:(p + 1) * LANES] = jnp.where(lane < SWA_HEAD_DIM, o_a, o_b).astype(BF16)


def _window_attention(att, sink, n_batch, n_lat, n_ctx, with_ctx):
    T = att.shape[0]
    group = SWA_Q_HEADS // SWA_KV_HEADS
    qw = group * SWA_HEAD_DIM
    n_lat_blocks = n_lat // SWA_BLOCK
    n_ctx_blocks = n_ctx // SWA_BLOCK if with_ctx else 0
    k_col = SWA_Q_HEADS * SWA_HEAD_DIM // LANES
    v_col = k_col + SWA_KV_HEADS
    ctx_row = n_batch * n_lat // n_ctx

    def q_map(b, kvh, nb):
        lat = b * n_lat_blocks + nb
        ctx = n_batch * n_lat_blocks + b * (n_ctx // SWA_BLOCK) + (nb - n_lat_blocks)
        return (jnp.where(nb < n_lat_blocks, lat, ctx), kvh)

    grid_spec = pltpu.PrefetchScalarGridSpec(
        num_scalar_prefetch=0,
        grid=(n_batch, SWA_KV_HEADS, n_lat_blocks + n_ctx_blocks),
        in_specs=[pl.BlockSpec(memory_space=pltpu.SMEM),
                  pl.BlockSpec((SWA_BLOCK, qw), q_map),
                  pl.BlockSpec((n_lat, LANES), lambda b, kvh, nb: (b, k_col + kvh)),
                  pl.BlockSpec((n_lat, LANES), lambda b, kvh, nb: (b, v_col + kvh)),
                  pl.BlockSpec((n_ctx, LANES), lambda b, kvh, nb: (ctx_row + b, k_col + kvh)),
                  pl.BlockSpec((n_ctx, LANES), lambda b, kvh, nb: (ctx_row + b, v_col + kvh))],
        out_specs=pl.BlockSpec((SWA_BLOCK, qw), q_map),
    )
    return pl.pallas_call(
        functools.partial(_swa_kernel, n_lat=n_lat, group=group),
        grid_spec=grid_spec,
        out_shape=jax.ShapeDtypeStruct((T, SWA_Q_HEADS * SWA_HEAD_DIM), BF16),
        compiler_params=_params(("arbitrary", "arbitrary", "arbitrary")), name="window_attention",
    )(sink.astype(F32), att, att, att, att, att)


DN_PAIR = 2 * DN_CHUNK
DN_GROUP = 4


def _bdot(a, b):
    return jnp.einsum('gik,gkj->gij', a, b, preferred_element_type=F32)


def _bdot_nt(a, b):
    return jnp.einsum('gik,gjk->gij', a, b, preferred_element_type=F32)


def _split2(x):
    hi = x.astype(BF16)
    return hi, (x - hi.astype(F32)).astype(BF16)


def _bdot3(a, b):
    ah, al = _split2(a)
    bh, bl = _split2(b)
    return _bdot(ah, bh) + (_bdot(ah, bl) + _bdot(al, bh))


def _softplus(x):
    return jnp.maximum(x, 0.0) + jnp.log(1.0 + jnp.exp(-jnp.abs(x)))


def _dn_prepare_direction(q, k, v, gc, beta, fwd):
    G = q.shape[0]
    ii = lax.broadcasted_iota(jnp.int32, (DN_PAIR, DN_PAIR), 0)
    jj = lax.broadcasted_iota(jnp.int32, (DN_PAIR, DN_PAIR), 1)
    same = (ii >> 6) == (jj >> 6)
    incl = same & ((ii >= jj) if fwd else (ii <= jj))
    strict = same & ((ii > jj) if fwd else (ii < jj))
    eye = (ii == jj).astype(F32)
    lane0 = jnp.broadcast_to((jj == 0).astype(BF16), (G, DN_PAIR, DN_PAIR))
    g_row = sum(_bdot_nt(lane0, p) for p in _split3(gc))
    decay = jnp.where(incl, jnp.exp(jnp.where(incl, gc - g_row, 0.0)), 0.0)
    kb = k * beta
    vb = v * beta
    k16 = k.astype(BF16)
    m = jnp.where(strict, _bdot_nt(kb.astype(BF16), k16) * decay, 0.0)
    inv = eye - m
    p = m
    for _ in range(5):
        p = _bdot3(p, p)
        inv = inv + _bdot3(inv, p)
    inv16 = inv.astype(BF16)
    eg = jnp.exp(gc)
    u = _bdot(inv16, vb.astype(BF16))
    w = _bdot(inv16, (kb * eg).astype(BF16))
    a_intra = jnp.where(incl, _bdot_nt(q.astype(BF16), k16) * decay, 0.0)
    lo, hi = (DN_CHUNK - 1, DN_PAIR - 1) if fwd else (0, DN_CHUNK)
    first_chunk = lax.broadcasted_iota(jnp.int32, (DN_PAIR, 1), 0) < DN_CHUNK
    g_last = jnp.where(first_chunk, gc[:, lo:lo + 1, :], gc[:, hi:hi + 1, :])
    kg = k * jnp.exp(g_last - gc)
    eye16 = jnp.broadcast_to(eye.astype(BF16), (G, DN_PAIR, DN_PAIR))
    kg_t = _bdot_nt(eye16, kg.astype(BF16))
    return u, w.astype(BF16), a_intra.astype(BF16), (q * eg).astype(BF16), kg_t.astype(BF16), jnp.exp(g_last)


def _dn_sequence(q_ref, k_ref, v_ref, z_ref, ab_ref, cw_refs, gpar_ref, nw_ref, o_ref, s0, head, T, sc):
    (qn, kn, vn, gcs, bts, us, ws, aas, qgs, kgts, egls, outs) = sc
    row = lax.broadcasted_iota(jnp.int32, (T, 1), 0)

    def conv_silu(x, w_ref):
        acc = x * w_ref[DN_CONV // 2:DN_CONV // 2 + 1, :]
        for j in range(DN_CONV):
            d = j - DN_CONV // 2
            if d == 0:
                continue
            shifted = pltpu.roll(x, (-d) % T, axis=0)
            ok = (row + d >= 0) & (row + d < T)
            acc = acc + jnp.where(ok, shifted, 0.0) * w_ref[j:j + 1, :]
        return acc * _sigmoid(acc)

    q = conv_silu(q_ref[...], cw_refs[0])
    k = conv_silu(k_ref[...], cw_refs[1])
    v = conv_silu(v_ref[...], cw_refs[2])
    q = q * lax.rsqrt(jnp.sum(q * q, axis=1, keepdims=True) + RMS_EPS) * DN_HEAD_DIM ** -0.5
    k = k * lax.rsqrt(jnp.sum(k * k, axis=1, keepdims=True) + RMS_EPS)
    qn[pl.ds(0, T), :] = q
    kn[pl.ds(0, T), :] = k
    vn[pl.ds(0, T), :] = v

    ab = ab_ref[...]
    g_all = -jnp.exp(gpar_ref[0:1, :]) * _softplus(ab + gpar_ref[1:2, :])
    b_all = _sigmoid(ab)
    col = lax.broadcasted_iota(jnp.int32, (LANES, LANES), 0)

    def column(x, c):
        pick = (col == c).astype(BF16)
        return sum(_dot(p, pick) for p in _split3(x))

    pos = row & (DN_CHUNK - 1)
    for d in range(2):
        g = column(g_all, d * DN_HEADS + head)
        for sh in (1, 2, 4, 8, 16, 32):
            if d == 0:
                g = g + jnp.where(pos >= sh, pltpu.roll(g, sh, axis=0), 0.0)
            else:
                g = g + jnp.where(pos < DN_CHUNK - sh, pltpu.roll(g, T - sh, axis=0), 0.0)
        gcs[d][pl.ds(0, T), :] = g
        bts[d][pl.ds(0, T), :] = column(b_all, 2 * DN_HEADS + d * DN_HEADS + head)

    n_pairs = T // DN_PAIR
    G = min(DN_GROUP, n_pairs)
    R = G * DN_PAIR

    def prepare(gi, carry):
        rows = pl.ds(pl.multiple_of(gi * R, R), R)
        shape = (G, DN_PAIR, LANES)
        qq = qn[rows, :].reshape(shape)
        kk = kn[rows, :].reshape(shape)
        vv = vn[rows, :].reshape(shape)
        for d in range(2):
            res = _dn_prepare_direction(qq, kk, vv, gcs[d][rows, :].reshape(shape), bts[d][rows, :].reshape(shape), d == 0)
            for ref, val in zip((us[d], ws[d], aas[d], qgs[d], kgts[d], egls[d]), res):
                ref[rows, :] = val.reshape(R, LANES)
        return carry

    lax.fori_loop(0, n_pairs // G, prepare, 0)

    zeros = jnp.zeros((DN_CHUNK, LANES), BF16)

    def chunk_step(S, base, half, d):
        off = pl.multiple_of(base + half * DN_CHUNK, DN_CHUNK)
        rows = pl.ds(off, DN_CHUNK)
        s16 = S.astype(BF16)
        v_new = us[d][rows, :] - _dot(ws[d][rows, :], s16)
        v16 = v_new.astype(BF16)
        v2 = jnp.concatenate([v16, zeros] if half == 0 else [zeros, v16], axis=0)
        outs[d][rows, :] = _dot(qgs[d][rows, :], s16) + _dot(aas[d][rows, :], v2)
        decay = jnp.broadcast_to(egls[d][pl.ds(off, 1), :], (DN_PAIR, LANES))
        return S * decay + _dot(kgts[d][pl.ds(pl.multiple_of(base, DN_PAIR), DN_PAIR), :], v2)

    def scan(p, carry):
        s_f, s_b = carry
        base_f = p * DN_PAIR
        base_b = (n_pairs - 1 - p) * DN_PAIR
        s_f = chunk_step(s_f, base_f, 0, 0)
        s_b = chunk_step(s_b, base_b, 1, 1)
        s_f = chunk_step(s_f, base_f, 1, 0)
        s_b = chunk_step(s_b, base_b, 0, 1)
        return s_f, s_b

    s_f, s_b = lax.fori_loop(0, n_pairs, scan, s0)

    o = outs[0][pl.ds(0, T), :] + outs[1][pl.ds(0, T), :]
    o = o * lax.rsqrt(jnp.mean(o * o, axis=1, keepdims=True) + RMS_EPS) * nw_ref[...]
    z = z_ref[...]
    o_ref[...] = (o * (z * _sigmoid(z))).astype(BF16)
    return s_f, s_b


def _dn_kernel(ql, kl, vl, zl, abl, qc, kc, vc, zc, abc, cwq, cwk, cwv, gpar, nw, ol, oc, *scratch, n_lat, n_ctx):
    head = pl.program_id(1)
    names = 12
    sc = []
    it = iter(scratch)
    for i in range(names):
        sc.append(next(it) if i < 3 else (next(it), next(it)))
    zero = jnp.zeros((DN_HEAD_DIM, DN_HEAD_DIM), F32)
    s_ctx = _dn_sequence(qc, kc, vc, zc, abc, (cwq, cwk, cwv), gpar, nw, oc, (zero, zero), head, n_ctx, sc)
    _dn_sequence(ql, kl, vl, zl, abl, (cwq, cwk, cwv), gpar, nw, ol, s_ctx, head, n_lat, sc)


def _delta_net(dn, conv_w, a_log, dt_bias, norm_w, n_batch, n_lat, n_ctx):
    H = DN_HEADS
    ctx_row = n_batch * n_lat // n_ctx
    ab_col = 4 * H

    def lat(off):
        return pl.BlockSpec((n_lat, LANES), lambda b, h: (b, off + h))

    def ctx(off):
        return pl.BlockSpec((n_ctx, LANES), lambda b, h: (ctx_row + b, off + h))

    cw = jnp.zeros((SUBLANES, 3 * DN_WIDTH), F32).at[:DN_CONV].set(conv_w.astype(F32))
    gpar = jnp.zeros((SUBLANES, LANES), F32)
    gpar = gpar.at[0, :2 * H].set(a_log.reshape(-1).astype(F32)).at[1, :2 * H].set(dt_bias.reshape(-1).astype(F32))
    in_specs = [lat(0), lat(H), lat(2 * H), lat(3 * H), pl.BlockSpec((n_lat, LANES), lambda b, h: (b, ab_col)),
                ctx(0), ctx(H), ctx(2 * H), ctx(3 * H), pl.BlockSpec((n_ctx, LANES), lambda b, h: (ctx_row + b, ab_col)),
                pl.BlockSpec((SUBLANES, LANES), lambda b, h: (0, h)),
                pl.BlockSpec((SUBLANES, LANES), lambda b, h: (0, H + h)),
                pl.BlockSpec((SUBLANES, LANES), lambda b, h: (0, 2 * H + h)),
                pl.BlockSpec((SUBLANES, LANES), lambda b, h: (0, 0)),
                pl.BlockSpec((1, LANES), lambda b, h: (0, 0))]
    f32s = lambda: pltpu.VMEM((n_lat, LANES), F32)
    b16s = lambda: pltpu.VMEM((n_lat, LANES), BF16)
    scratch = [f32s(), f32s(), f32s()]
    scratch += [f32s() for _ in range(4)]
    scratch += [f32s(), f32s()]
    scratch += [b16s() for _ in range(8)]
    scratch += [f32s() for _ in range(4)]
    return pl.pallas_call(
        functools.partial(_dn_kernel, n_lat=n_lat, n_ctx=n_ctx),
        grid=(n_batch, H),
        in_specs=in_specs,
        out_specs=[pl.BlockSpec((n_lat, LANES), lambda b, h: (b, h)),
                   pl.BlockSpec((n_ctx, LANES), lambda b, h: (b, h))],
        out_shape=[jax.ShapeDtypeStruct((n_batch * n_lat, DN_WIDTH), BF16),
                   jax.ShapeDtypeStruct((n_batch * n_ctx, DN_WIDTH), BF16)],
        scratch_shapes=scratch,
        compiler_params=_params(("arbitrary", "arbitrary")), name="delta_net",
    )(dn, dn, dn, dn, dn, dn, dn, dn, dn, dn, cw, cw, cw, gpar, norm_w.reshape(1, -1).astype(F32))


def _rope_tables(n_tokens):
    t = jnp.arange(n_tokens)
    n_freq = SWA_HEAD_DIM // 4
    inv = ROPE_THETA ** (-jnp.arange(n_freq, dtype=F32) / n_freq)
    row = (t // GRID_W).astype(F32)[:, None]
    col = (t % GRID_W).astype(F32)[:, None]
    ang = jnp.concatenate([row * inv, col * inv], -1)
    cos, sin = jnp.cos(ang), jnp.sin(ang)
    cos_t = jnp.concatenate([cos, cos, cos, cos], -1)
    sin_t = jnp.concatenate([-sin, -sin, sin, sin], -1)
    ident = jnp.ones((TOK_TILE, LANES), F32)
    return (jnp.concatenate([cos_t, ident], 0), jnp.concatenate([sin_t, jnp.zeros_like(ident)], 0))


def _swa_weight(w_in):
    D = w_in.shape[0]
    qw = SWA_Q_HEADS * SWA_HEAD_DIM
    kw = SWA_KV_HEADS * SWA_HEAD_DIM
    half = SWA_HEAD_DIM // 2
    wq = w_in[:, :qw].reshape(D, SWA_Q_HEADS // 2, 2, 2, half)
    wq = jnp.transpose(wq, (0, 1, 3, 2, 4)).reshape(D, qw)
    wk = w_in[:, qw:qw + kw].reshape(D, SWA_KV_HEADS, 2, 1, half)
    wk = jnp.broadcast_to(wk, (D, SWA_KV_HEADS, 2, 2, half)).reshape(D, 2 * kw)
    wv = w_in[:, qw + kw:].reshape(D, SWA_KV_HEADS, 1, SWA_HEAD_DIM)
    wv = jnp.broadcast_to(wv, (D, SWA_KV_HEADS, 2, SWA_HEAD_DIM)).reshape(D, 2 * kw)
    return jnp.concatenate([wq, wk, wv], axis=1).astype(BF16)


def kernel(x, c, ctx, c_ctx, w_mod, b_mod, ln_g, ln_b, w_in_ab, na_rpb, dn_conv, dn_a_log, dn_dt_bias, dn_norm_w,
           w_out_ab, w_in_c, swa_sink, w_out_c, w_router, b_router, w_gu, b_gu, w_down, b_down):
    B, N, D = x.shape
    L = ctx.shape[1]
    depth = w_mod.shape[0]
    alpha = (2 * depth) ** 0.25
    assert D == D_MODEL and N % TOK_TILE == 0 and L == TOK_TILE
    geom = (B * N // TOK_TILE, N // TOK_TILE, B)

    mod_rows = -(-(B + 1) // 16) * 16
    cc = jnp.zeros((mod_rows, D), F32).at[:B].set(c).at[B].set(c_ctx)
    mod_all = _modulation(cc, w_mod, b_mod)
    cos_t, sin_t = _rope_tables(N)
    n_att_ab = 3 * NA_WIDTH
    ab_main = n_att_ab + 4 * DN_WIDTH
    wr = jnp.zeros((depth, D, LANES), BF16).at[:, :, :N_EXPERTS].set(w_router.astype(BF16))
    br = jnp.full((depth, 1, LANES), NEG_INF, F32).at[:, 0, :N_EXPERTS].set(b_router.astype(F32))

    h = jnp.concatenate([x.reshape(B * N, D), ctx.reshape(B * L, D)], axis=0)
    moe_prev = mod_prev = ln_prev = None
    out = None
    for layer in range(depth):
        last = layer == depth - 1
        i = layer // 2
        mod = mod_all[layer].reshape(mod_rows, 1, -1)
        if layer % 2 == 0:
            w = w_in_ab[i]
            w = jnp.concatenate([w[:, :ab_main], jnp.pad(w[:, ab_main:], ((0, 0), (0, LANES - (w.shape[1] - ab_main))))], 1)
            h, att, dn = _proj_in(h, moe_prev, mod_prev, ln_prev, mod, w.astype(BF16), geom, n_att=n_att_ab, alpha=alpha)
            o_a = _neighbourhood_attention(att, _na_bias_table(na_rpb[i], N // GRID_W), B, N, L)
            o_bl, o_bc = _delta_net(dn, dn_conv[i], dn_a_log[i], dn_dt_bias[i], dn_norm_w[i], B, N, L)
            xs = [o_a, jnp.concatenate([o_bl, o_bc], axis=0)]
            wo = w_out_ab[i].astype(BF16)
            ws = [wo[:NA_WIDTH], wo[NA_WIDTH:]]
        else:
            w = _swa_weight(w_in_c[i])
            n_rope = (SWA_Q_HEADS + 2 * SWA_KV_HEADS) * SWA_HEAD_DIM // LANES
            h, att, _ = _proj_in(h, moe_prev, mod_prev, ln_prev, mod, w, geom, n_att=w.shape[1],
                                 rope=(cos_t, sin_t, n_rope), alpha=alpha)
            xs = [_window_attention(att, swa_sink[i], B, N, L, with_ctx=not last)]
            ws = [w_out_c[i].astype(BF16)]
        n_rows = B * N if last else B * (N + L)
        ln1 = jnp.stack([ln_g[layer, 0], ln_b[layer, 0]])
        ln2 = jnp.stack([ln_g[layer, 1], ln_b[layer, 1]])
        h1, f_moe, idx, gates = _proj_out(xs, ws, h, mod, ln1, wr[layer], br[layer], geom, n_rows, alpha=alpha)
        y_moe = _moe(f_moe, idx, gates, w_gu[layer].astype(BF16), b_gu[layer], w_down[layer].astype(BF16), b_down[layer])
        if last:
            out = _final_combine(h1, y_moe, mod, ln2, geom, alpha=alpha).reshape(B, N, D)
        else:
            h, moe_prev, mod_prev, ln_prev = h1, y_moe, mod, ln2
    return out
```

```python
import functools

import jax
import jax.numpy as jnp
import numpy as np
from jax import lax
from jax.experimental import pallas as pl
from jax.experimental.pallas import tpu as pltpu

F32 = jnp.float32
BF16 = jnp.bfloat16

D_MODEL = 1024
GRID_W = 64
NA_HEADS = 8
NA_HEAD_DIM = 64
NA_WIN_H = 8
NA_WIN_W = 16
DN_HEADS = 4
DN_HEAD_DIM = 128
DN_CONV = 5
DN_CHUNK = 64
SWA_Q_HEADS = 16
SWA_KV_HEADS = 2
SWA_HEAD_DIM = 64
SWA_WINDOW = 128
ROPE_THETA = 10000.0
N_EXPERTS = 32
TOP_K = 4
D_EXPERT = 1024
SWIGLU_LIMIT = 7.0
SWIGLU_ALPHA = 1.702
LN_EPS = 1e-5
RMS_EPS = 1e-6
NEG_INF = -1e30

LANES = 128
SUBLANES = 8
TOK_TILE = 256
MOE_TILE = 128
MOE_CHUNK_MAX = 3072
VMEM_LIMIT = 56 * 1024 * 1024

NA_WIDTH = NA_HEADS * NA_HEAD_DIM
DN_WIDTH = DN_HEADS * DN_HEAD_DIM
NA_QROWS = 4
NA_KROWS = 12


def _params(sem):
    return pltpu.CompilerParams(dimension_semantics=sem, vmem_limit_bytes=VMEM_LIMIT)


def _sigmoid(x):
    return 1.0 / (1.0 + jnp.exp(-x))


def _dot(a, b):
    return jnp.dot(a, b, preferred_element_type=F32)


def _dot_nt(a, b):
    return lax.dot_general(a, b, (((1,), (1,)), ((), ())), preferred_element_type=F32)


def _moe_layout_load(ref, rows):
    return jnp.concatenate([ref[pl.ds(s, rows, stride=SUBLANES), :] for s in range(SUBLANES)], axis=1)


def _moe_layout_store(ref, val, rows):
    for s in range(SUBLANES):
        ref[pl.ds(s, rows, stride=SUBLANES), :] = val[:, s * LANES:(s + 1) * LANES]


def _layer_norm(y, g, b):
    mu = jnp.mean(y, axis=-1, keepdims=True)
    d = y - mu
    var = jnp.mean(d * d, axis=-1, keepdims=True)
    return d * lax.rsqrt(var + LN_EPS) * g + b


def _mod_kernel(s_ref, w_ref, b_ref, o_ref):
    s = s_ref[...]
    s = s * _sigmoid(s)
    o_ref[0] = _dot(s.astype(BF16), w_ref[0].astype(BF16)) + b_ref[0]


def _modulation(cc, w_mod, b_mod):
    depth = w_mod.shape[0]
    rows = cc.shape[0]
    nblk = w_mod.shape[2] // D_MODEL
    return pl.pallas_call(
        _mod_kernel,
        grid=(depth, nblk),
        in_specs=[pl.BlockSpec((rows, D_MODEL), lambda l, j: (0, 0)),
                  pl.BlockSpec((1, D_MODEL, D_MODEL), lambda l, j: (l, 0, j)),
                  pl.BlockSpec((1, 1, D_MODEL), lambda l, j: (l, 0, j))],
        out_specs=pl.BlockSpec((1, rows, D_MODEL), lambda l, j: (l, 0, j)),
        out_shape=jax.ShapeDtypeStruct((depth, rows, nblk * D_MODEL), F32),
        compiler_params=_params(("arbitrary", "arbitrary")),
        name="modulation",
    )(cc, w_mod, b_mod.reshape(depth, 1, -1))


def _pin_kernel(*refs, first, n_rope, n_att, alpha):
    refs = list(refs)
    h_ref = refs.pop(0)
    if not first:
        moe_ref, g2_ref, lg_ref, lb_ref = refs[:4]
        refs = refs[4:]
    sc_ref, sh_ref, w_ref = refs[:3]
    refs = refs[3:]
    if n_rope:
        cos_ref, sin_ref = refs[:2]
        refs = refs[2:]
    if not first:
        hout_ref = refs.pop(0)
    att_ref = refs.pop(0)
    dn_ref = refs.pop(0) if refs else None

    h = h_ref[...]
    if not first:
        m = _moe_layout_load(moe_ref, TOK_TILE)
        h = _layer_norm(alpha * h + g2_ref[0] * m, lg_ref[...], lb_ref[...])
        hout_ref[...] = h
    a = h * (1.0 + sc_ref[0]) + sh_ref[0]
    acc = _dot(a.astype(BF16), w_ref[...])
    if n_rope:
        cos = cos_ref[...]
        sin = sin_ref[...]
        for g in range(n_att // LANES):
            blk = acc[:, g * LANES:(g + 1) * LANES]
            if g < n_rope:
                blk = blk * cos + pltpu.roll(blk, LANES // 2, axis=1) * sin
            att_ref[:, g * LANES:(g + 1) * LANES] = blk.astype(BF16)
    else:
        att_ref[...] = acc[:, :n_att].astype(BF16)
    if dn_ref is not None:
        dn_ref[...] = acc[:, n_att:]


def _mod_spec(k, n_lat_tiles, lat_tiles_per_batch, n_batch):
    def imap(i):
        return (jnp.where(i < n_lat_tiles, i // lat_tiles_per_batch, n_batch), 0, k)
    return pl.BlockSpec((1, 1, D_MODEL), imap)


def _proj_in(h, moe_prev, mod_prev, ln_prev, mod, w, geom, *, n_att, rope=None, alpha=1.0):
    T = h.shape[0]
    n_tiles = T // TOK_TILE
    n_lat_tiles, per_batch, n_batch = geom
    first = moe_prev is None
    ncols = w.shape[1]
    row = pl.BlockSpec((TOK_TILE, D_MODEL), lambda i: (i, 0))
    vec = pl.BlockSpec((1, D_MODEL), lambda i: (0, 0))
    ms = functools.partial(_mod_spec, n_lat_tiles=n_lat_tiles, lat_tiles_per_batch=per_batch, n_batch=n_batch)
    args, specs = [h], [row]
    if not first:
        args += [moe_prev, mod_prev, ln_prev[0:1], ln_prev[1:2]]
        specs += [pl.BlockSpec((TOK_TILE * SUBLANES, LANES), lambda i: (i, 0)), ms(5), vec, vec]
    args += [mod, mod, w]
    specs += [ms(1), ms(0), pl.BlockSpec((D_MODEL, ncols), lambda i: (0, 0))]
    n_rope = 0
    if rope is not None:
        cos_t, sin_t, n_rope = rope
        tab = pl.BlockSpec((TOK_TILE, LANES), lambda i: (jnp.where(i < n_lat_tiles, i % per_batch, per_batch), 0))
        args += [cos_t, sin_t]
        specs += [tab, tab]
    out_shapes, out_specs = [], []
    if not first:
        out_shapes.append(jax.ShapeDtypeStruct((T, D_MODEL), F32))
        out_specs.append(row)
    out_shapes.append(jax.ShapeDtypeStruct((T, n_att), BF16))
    out_specs.append(pl.BlockSpec((TOK_TILE, n_att), lambda i: (i, 0)))
    if ncols > n_att:
        out_shapes.append(jax.ShapeDtypeStruct((T, ncols - n_att), F32))
        out_specs.append(pl.BlockSpec((TOK_TILE, ncols - n_att), lambda i: (i, 0)))
    outs = pl.pallas_call(
        functools.partial(_pin_kernel, first=first, n_rope=n_rope, n_att=n_att, alpha=alpha),
        grid=(n_tiles,), in_specs=specs, out_specs=out_specs, out_shape=out_shapes,
        compiler_params=_params(("arbitrary",)), name="proj_in",
    )(*args)
    outs = list(outs)
    h_new = h if first else outs.pop(0)
    att = outs.pop(0)
    dn = outs.pop(0) if outs else None
    return h_new, att, dn


def _final_kernel(h_ref, moe_ref, g2_ref, lg_ref, lb_ref, o_ref, *, alpha):
    m = _moe_layout_load(moe_ref, TOK_TILE)
    o_ref[...] = _layer_norm(alpha * h_ref[...] + g2_ref[0] * m, lg_ref[...], lb_ref[...])


def _final_combine(h, moe, mod, ln, geom, *, alpha):
    T = h.shape[0]
    n_lat_tiles, per_batch, n_batch = geom
    row = pl.BlockSpec((TOK_TILE, D_MODEL), lambda i: (i, 0))
    vec = pl.BlockSpec((1, D_MODEL), lambda i: (0, 0))
    return pl.pallas_call(
        functools.partial(_final_kernel, alpha=alpha),
        grid=(T // TOK_TILE,),
        in_specs=[row, pl.BlockSpec((TOK_TILE * SUBLANES, LANES), lambda i: (i, 0)),
                  _mod_spec(5, n_lat_tiles, per_batch, n_batch), vec, vec],
        out_specs=row, out_shape=jax.ShapeDtypeStruct((T, D_MODEL), F32),
        compiler_params=_params(("arbitrary",)), name="final_combine",
    )(h, moe, mod, ln[0:1], ln[1:2])


def _pout_kernel(*refs, n_in, alpha):
    xs = refs[:n_in]
    ws = refs[n_in:2 * n_in]
    (h_ref, g1_ref, lg_ref, lb_ref, sc_ref, sh_ref, wr_ref, br_ref,
     hout_ref, f_ref, idx_ref, gate_ref) = refs[2 * n_in:]
    m = _dot(xs[0][...], ws[0][...])
    for x_ref, w_ref in zip(xs[1:], ws[1:]):
        m = m + _dot(x_ref[...], w_ref[...])
    h1 = _layer_norm(alpha * h_ref[...] + g1_ref[0] * m, lg_ref[...], lb_ref[...])
    hout_ref[...] = h1
    f = h1 * (1.0 + sc_ref[0]) + sh_ref[0]
    _moe_layout_store(f_ref, f, TOK_TILE)
    logits = _dot(f.astype(BF16), wr_ref[...]) + br_ref[...]
    lane = lax.broadcasted_iota(jnp.int32, logits.shape, 1)
    work = logits
    vals, idxs = [], []
    for _ in range(TOP_K):
        v = jnp.max(work, axis=1, keepdims=True)
        ix = jnp.min(jnp.where(work == v, lane, LANES), axis=1, keepdims=True)
        vals.append(v)
        idxs.append(ix)
        work = jnp.where(lane == ix, -jnp.inf, work)
    es = [jnp.exp(v - vals[0]) for v in vals]
    den = es[0]
    for e in es[1:]:
        den = den + e
    idx_out = jnp.zeros(logits.shape, jnp.int32)
    gate_out = jnp.zeros(logits.shape, F32)
    for k in range(TOP_K):
        idx_out = jnp.where(lane == k, idxs[k], idx_out)
        gate_out = jnp.where(lane == k, es[k] / den, gate_out)
    idx_ref[...] = idx_out
    gate_ref[...] = gate_out


def _proj_out(xs, ws, h, mod, ln, w_router, b_router, geom, n_rows, *, alpha):
    n_lat_tiles, per_batch, n_batch = geom
    row = pl.BlockSpec((TOK_TILE, D_MODEL), lambda i: (i, 0))
    vec = pl.BlockSpec((1, D_MODEL), lambda i: (0, 0))
    lane_row = pl.BlockSpec((TOK_TILE, LANES), lambda i: (i, 0))
    ms = functools.partial(_mod_spec, n_lat_tiles=n_lat_tiles, lat_tiles_per_batch=per_batch, n_batch=n_batch)
    specs = [pl.BlockSpec((TOK_TILE, x.shape[1]), lambda i: (i, 0)) for x in xs]
    specs += [pl.BlockSpec(w.shape, lambda i: (0, 0)) for w in ws]
    specs += [row, ms(2), vec, vec, ms(4), ms(3),
              pl.BlockSpec((D_MODEL, LANES), lambda i: (0, 0)), pl.BlockSpec((1, LANES), lambda i: (0, 0))]
    return pl.pallas_call(
        functools.partial(_pout_kernel, n_in=len(xs), alpha=alpha),
        grid=(n_rows // TOK_TILE,), in_specs=specs,
        out_specs=[row, pl.BlockSpec((TOK_TILE * SUBLANES, LANES), lambda i: (i, 0)), lane_row, lane_row],
        out_shape=[jax.ShapeDtypeStruct((n_rows, D_MODEL), F32),
                   jax.ShapeDtypeStruct((n_rows * SUBLANES, LANES), F32),
                   jax.ShapeDtypeStruct((n_rows, LANES), jnp.int32),
                   jax.ShapeDtypeStruct((n_rows, LANES), F32)],
        compiler_params=_params(("arbitrary",)), name="proj_out",
    )(*xs, *ws, h, mod, ln[0:1], ln[1:2], mod, mod, w_router, b_router)


def _split3(x):
    hi = x.astype(BF16)
    r = x - hi.astype(F32)
    mid = r.astype(BF16)
    lo = (r - mid.astype(F32)).astype(BF16)
    return hi, mid, lo


MOE_UNROLL = 8


def _moe_kernel(goff_ref, cnt_ref, tok_hbm, gw_hbm, f_ref, wgu_ref, bgu_ref, wd_ref, bd_ref,
                o_ref, xg_ref, tok_s, gw_s, sem, *, n_assign):
    c = pl.program_id(0)
    e = pl.program_id(1)

    @pl.when(e == 0)
    def _():
        src = pl.ds(pl.multiple_of(c * n_assign, 1024), n_assign)
        cp_tok = pltpu.make_async_copy(tok_hbm.at[src], tok_s, sem.at[0])
        cp_gw = pltpu.make_async_copy(gw_hbm.at[src], gw_s, sem.at[1])
        cp_tok.start()
        cp_gw.start()
        o_ref[...] = jnp.zeros(o_ref.shape, F32)
        cp_tok.wait()
        cp_gw.wait()

    @pl.when((c == 0) & (e == 0))
    def _():
        xg_ref[...] = jnp.zeros(xg_ref.shape, F32)

    g = c * N_EXPERTS + e
    group0 = goff_ref[g]
    n_rows = cnt_ref[g]

    def rows8(i):
        return pl.ds(pl.multiple_of(i * SUBLANES, SUBLANES), SUBLANES)

    def tile(r, carry):
        off = group0 + r * MOE_TILE
        n_valid = jnp.minimum(MOE_TILE, n_rows - r * MOE_TILE)
        n_main = n_valid // MOE_UNROLL

        def gather_row(j):
            xg_ref[rows8(j), :] = f_ref[rows8(tok_s[off + j]), :]

        def gather_main(i, _):
            for u in range(MOE_UNROLL):
                gather_row(i * MOE_UNROLL + u)
            return 0

        def gather_tail(j, _):
            gather_row(j)
            return 0

        lax.fori_loop(0, n_main, gather_main, 0)
        lax.fori_loop(n_main * MOE_UNROLL, n_valid, gather_tail, 0)
        x = _moe_layout_load(xg_ref, MOE_TILE).astype(BF16)
        gu = _dot(x, wgu_ref[0, 0]) + bgu_ref[0, 0]
        gate = jnp.minimum(gu[:, :D_EXPERT], SWIGLU_LIMIT)
        up = jnp.clip(gu[:, D_EXPERT:], -SWIGLU_LIMIT, SWIGLU_LIMIT)
        hid = gate * _sigmoid(SWIGLU_ALPHA * gate) * (up + 1.0)
        y = _dot(hid.astype(BF16), wd_ref[0, 0]) + bd_ref[0, 0]
        _moe_layout_store(xg_ref, y, MOE_TILE)

        def scatter_row(j):
            dst = rows8(tok_s[off + j])
            o_ref[dst, :] = o_ref[dst, :] + gw_s[off + j] * xg_ref[rows8(j), :]

        def scatter_main(i, _):
            js = [i * MOE_UNROLL + u for u in range(MOE_UNROLL)]
            dsts = [rows8(tok_s[off + j]) for j in js]
            olds = [o_ref[d, :] for d in dsts]
            for j, d, old in zip(js, dsts, olds):
                o_ref[d, :] = old + gw_s[off + j] * xg_ref[rows8(j), :]
            return 0

        def scatter_tail(j, _):
            scatter_row(j)
            return 0

        lax.fori_loop(0, n_main, scatter_main, 0)
        lax.fori_loop(n_main * MOE_UNROLL, n_valid, scatter_tail, 0)
        return carry

    lax.fori_loop(0, (n_rows + MOE_TILE - 1) // MOE_TILE, tile, 0)


def _moe_chunk(n_rows):
    c = MOE_CHUNK_MAX
    while n_rows % c:
        c -= TOK_TILE
    return c


def _moe_metadata(idx, gates, chunk):
    n_rows = idx.shape[0]
    n_chunks = n_rows // chunk
    n_assign = chunk * TOP_K
    e = idx[:, :TOP_K].reshape(n_chunks, n_assign)
    gv = gates[:, :TOP_K].reshape(n_chunks, n_assign)
    n = jnp.broadcast_to(jnp.arange(n_assign, dtype=jnp.int32), (n_chunks, n_assign))
    _, n_sorted, gw = lax.sort((e, n, gv), dimension=1, num_keys=1, is_stable=True)
    cnt = jnp.sum((e[:, :, None] == jnp.arange(N_EXPERTS, dtype=jnp.int32)).astype(jnp.int32), axis=1)
    goff = jnp.cumsum(cnt, axis=1) - cnt
    return (n_sorted // TOP_K).reshape(-1), gw.reshape(-1), goff.reshape(-1), cnt.reshape(-1)


def _moe(f_moe, idx, gates, wgu, bgu, wd, bd, layer):
    n_rows = idx.shape[0]
    chunk = _moe_chunk(n_rows)
    n_assign = chunk * TOP_K
    assert n_assign % 1024 == 0
    tok, gw, goff, cnt = _moe_metadata(idx, gates, chunk)
    resident = pl.BlockSpec((chunk * SUBLANES, LANES), lambda c, e, *_: (c, 0), pipeline_mode=pl.Buffered(1))
    hbm = pl.BlockSpec(memory_space=pl.ANY)
    grid_spec = pltpu.PrefetchScalarGridSpec(
        num_scalar_prefetch=2,
        grid=(n_rows // chunk, N_EXPERTS),
        in_specs=[hbm, hbm, resident,
                  pl.BlockSpec((1, 1, D_MODEL, 2 * D_EXPERT), lambda c, e, *_: (layer, e, 0, 0)),
                  pl.BlockSpec((1, 1, 1, 2 * D_EXPERT), lambda c, e, *_: (layer, e, 0, 0)),
                  pl.BlockSpec((1, 1, D_EXPERT, D_MODEL), lambda c, e, *_: (layer, e, 0, 0)),
                  pl.BlockSpec((1, 1, 1, D_MODEL), lambda c, e, *_: (layer, e, 0, 0))],
        out_specs=pl.BlockSpec((chunk * SUBLANES, LANES), lambda c, e, *_: (c, 0), pipeline_mode=pl.Buffered(1)),
        scratch_shapes=[pltpu.VMEM((MOE_TILE * SUBLANES, LANES), F32),
                        pltpu.SMEM((n_assign,), jnp.int32), pltpu.SMEM((n_assign,), F32),
                        pltpu.SemaphoreType.DMA((2,))],
    )
    return pl.pallas_call(
        functools.partial(_moe_kernel, n_assign=n_assign),
        grid_spec=grid_spec,
        out_shape=jax.ShapeDtypeStruct((n_rows * SUBLANES, LANES), F32),
        compiler_params=_params(("arbitrary", "arbitrary")), name="moe",
    )(goff, cnt, tok, gw, f_moe, wgu, bgu, wd, bd)


ATT_ROWS = 32
ATT_ROWS_UNROLL = 4


def _na_kernel(q_ref, k_ref, v_ref, kc_ref, vc_ref, bias_ref, o_ref, *, n_blocks, n_keys):
    blk = pl.program_id(2)
    key_row0 = jnp.clip(NA_QROWS * blk - NA_WIN_H // 2, 0, n_blocks * NA_QROWS - NA_KROWS)
    start = pl.multiple_of(key_row0 * GRID_W, GRID_W)
    q = q_ref[...]
    kl = k_ref[pl.ds(start, n_keys), :]
    vl = v_ref[pl.ds(start, n_keys), :]
    kc = kc_ref[...]
    vc = vc_ref[...]
    lane = lax.broadcasted_iota(jnp.int32, (1, LANES), 1)
    scale = NA_HEAD_DIM ** -0.5
    outs = []
    for a in range(2):
        sel = (lane < NA_HEAD_DIM) if a == 0 else (lane >= NA_HEAD_DIM)
        qa = jnp.where(sel, q, jnp.zeros_like(q)) * scale
        s_lat = _dot_nt(qa, kl) + bias_ref[0, a]
        s_ctx = _dot_nt(qa, kc)
        m = jnp.maximum(jnp.max(s_lat, axis=1, keepdims=True), jnp.max(s_ctx, axis=1, keepdims=True))
        e_lat = jnp.exp(s_lat - m)
        e_ctx = jnp.exp(s_ctx - m)
        den = jnp.sum(e_lat, axis=1, keepdims=True) + jnp.sum(e_ctx, axis=1, keepdims=True)
        outs.append((_dot(e_lat.astype(BF16), vl) + _dot(e_ctx.astype(BF16), vc)) / den)
    o_ref[...] = jnp.where(lane < NA_HEAD_DIM, outs[0], outs[1]).astype(BF16)


def _na_bias_table(rpb, rows):
    H = rpb.shape[0]
    qc = np.arange(GRID_W)
    col_start = np.clip(qc - NA_WIN_W // 2, 0, GRID_W - NA_WIN_W)
    col_in = (qc[None, :] >= col_start[:, None]) & (qc[None, :] < col_start[:, None] + NA_WIN_W)
    dx = np.clip(qc[None, :] - qc[:, None], 1 - NA_WIN_W, NA_WIN_W - 1) + NA_WIN_W - 1
    n_blocks = rows // NA_QROWS
    pick_y, masks = [], []
    for blk in (0, 1, n_blocks - 1):
        r = NA_QROWS * blk + np.arange(NA_QROWS)
        key_row0 = int(np.clip(NA_QROWS * blk - NA_WIN_H // 2, 0, rows - NA_KROWS))
        kr = key_row0 + np.arange(NA_KROWS)
        win0 = np.clip(r - NA_WIN_H // 2, 0, rows - NA_WIN_H)
        row_in = (kr[None, :] >= win0[:, None]) & (kr[None, :] < win0[:, None] + NA_WIN_H)
        dy = np.clip(kr[None, :] - r[:, None] + NA_WIN_H - 1, 0, 2 * NA_WIN_H - 2)
        pick_y.append(np.eye(2 * NA_WIN_H - 1, dtype=np.float32)[dy.reshape(-1)])
        masks.append(row_in[:, None, :, None] & col_in[None, :, None, :])
    pick_x = np.eye(2 * NA_WIN_W - 1, dtype=np.float32)[dx.reshape(-1)]
    bias = jnp.einsum('hyx,cay,bx->chab', rpb.astype(F32), jnp.asarray(np.stack(pick_y)), jnp.asarray(pick_x),
                      precision=lax.Precision.HIGHEST)
    bias = bias.reshape(3, H, NA_QROWS, NA_KROWS, GRID_W, GRID_W).transpose(0, 1, 2, 4, 3, 5)
    tabs = jnp.where(jnp.asarray(np.stack(masks))[:, None], bias, NEG_INF)
    tabs = tabs.reshape(3, H, NA_QROWS * GRID_W, NA_KROWS * GRID_W)
    return jnp.concatenate([tabs, jnp.full_like(tabs[:1], NEG_INF)], axis=0)


def _neighbourhood_attention(att, bias_tab, n_batch, n_lat, n_ctx):
    T = att.shape[0]
    qrows = NA_QROWS * GRID_W
    n_blocks = n_lat // qrows
    n_keys = NA_KROWS * GRID_W
    pairs = NA_WIDTH // LANES
    assert n_ctx == qrows
    lat_blocks = n_batch * n_blocks

    def q_map(b, hp, blk):
        return (jnp.where(blk < n_blocks, b * n_blocks + blk, lat_blocks + b), hp)

    def cls_map(b, hp, blk):
        cls = jnp.where(blk == 0, 0, jnp.where(blk == n_blocks - 1, 2, jnp.where(blk == n_blocks, 3, 1)))
        return (cls, hp, 0, 0)

    ctx_row = n_batch * n_lat // n_ctx
    return pl.pallas_call(
        functools.partial(_na_kernel, n_blocks=n_blocks, n_keys=n_keys),
        grid=(n_batch, pairs, n_blocks + 1),
        in_specs=[pl.BlockSpec((qrows, LANES), q_map),
                  pl.BlockSpec((n_lat, LANES), lambda b, hp, blk: (b, pairs + hp)),
                  pl.BlockSpec((n_lat, LANES), lambda b, hp, blk: (b, 2 * pairs + hp)),
                  pl.BlockSpec((n_ctx, LANES), lambda b, hp, blk: (ctx_row + b, pairs + hp)),
                  pl.BlockSpec((n_ctx, LANES), lambda b, hp, blk: (ctx_row + b, 2 * pairs + hp)),
                  pl.BlockSpec((1, 2, qrows, n_keys), cls_map)],
        out_specs=pl.BlockSpec((qrows, LANES), q_map),
        out_shape=jax.ShapeDtypeStruct((T, NA_WIDTH), BF16),
        compiler_params=_params(("arbitrary", "arbitrary", "arbitrary")), name="na_attention",
    )(att, att, att, att, att, bias_tab)


SWA_BLOCK = 128
SWA_BAND = 3 * SWA_BLOCK
def _swa_mask_table(n_lat):
    r = np.arange(SWA_BLOCK)[:, None]
    j = np.arange(SWA_BAND)[None, :]
    tabs = []
    n_blocks = n_lat // SWA_BLOCK
    for nb in (0, 1, n_blocks - 1):
        start = int(np.clip((nb - 1) * SWA_BLOCK, 0, n_lat - SWA_BAND))
        delta = nb * SWA_BLOCK + r - (start + j)
        tabs.append(np.where(np.abs(delta) <= SWA_WINDOW, 0.0, NEG_INF))
    tabs.append(np.full((SWA_BLOCK, SWA_BAND), NEG_INF))
    return jnp.asarray(np.stack(tabs), F32)


def _swa_kernel(sink_ref, q_ref, k_ref, v_ref, kc_ref, vc_ref, mask_ref, o_ref, s_ref, p_ref, den_ref, *, n_lat, group):
    kvh = pl.program_id(1)
    nb = pl.program_id(2)
    start = pl.multiple_of(jnp.clip((nb - 1) * SWA_BLOCK, 0, n_lat - SWA_BAND), SWA_BLOCK)
    kl = k_ref[pl.ds(start, SWA_BAND), :]
    vl = v_ref[pl.ds(start, SWA_BAND), :]
    q = q_ref[...]
    lane = lax.broadcasted_iota(jnp.int32, (1, LANES), 1)
    first_head = (lane & (SWA_HEAD_DIM - 1)) < SWA_HEAD_DIM // 2
    scale = SWA_HEAD_DIM ** -0.5
    parts = []
    for g in range(group):
        qp = q[:, (g // 2) * LANES:(g // 2 + 1) * LANES]
        sel = first_head if g % 2 == 0 else jnp.logical_not(first_head)
        parts.append(jnp.where(sel, qp, jnp.zeros_like(qp)) * scale)
    qs = jnp.concatenate(parts, axis=0)
    s_ref[:, :SWA_BAND] = _dot_nt(qs, kl)
    s_ref[:, SWA_BAND:] = _dot_nt(qs, kc_ref[...])

    def softmax_rows(i, carry):
        for u in range(ATT_ROWS_UNROLL):
            r0 = pl.multiple_of((i * ATT_ROWS_UNROLL + u) * ATT_ROWS, ATT_ROWS)
            rows = pl.ds(r0, ATT_ROWS)
            mrows = pl.ds(pl.multiple_of(r0 & (SWA_BLOCK - 1), ATT_ROWS), ATT_ROWS)
            s_lat = s_ref[rows, :SWA_BAND] + mask_ref[0, mrows, :]
            s_ctx = s_ref[rows, SWA_BAND:]
            sink = jnp.full((ATT_ROWS, 1), sink_ref[kvh * group + r0 // SWA_BLOCK], F32)
            m = jnp.maximum(jnp.maximum(jnp.max(s_lat, axis=1, keepdims=True), jnp.max(s_ctx, axis=1, keepdims=True)), sink)
            e_lat = jnp.exp(s_lat - m)
            e_ctx = jnp.exp(s_ctx - m)
            den = jnp.sum(e_lat, axis=1, keepdims=True) + jnp.sum(e_ctx, axis=1, keepdims=True) + jnp.exp(sink - m)
            p_ref[rows, :SWA_BAND] = e_lat.astype(BF16)
            p_ref[rows, SWA_BAND:] = e_ctx.astype(BF16)
            den_ref[rows, :] = jnp.broadcast_to(den, (ATT_ROWS, LANES))
        return carry

    lax.fori_loop(0, group * SWA_BLOCK // (ATT_ROWS * ATT_ROWS_UNROLL), softmax_rows, 0)
    o = (_dot(p_ref[:, :SWA_BAND], vl) + _dot(p_ref[:, SWA_BAND:], vc_ref[...])) / den_ref[...]
    for p in range(group // 2):
        o_a = o[(2 * p) * SWA_BLOCK:(2 * p + 1) * SWA_BLOCK]
        o_b = o[(2 * p + 1) * SWA_BLOCK:(2 * p + 2) * SWA_BLOCK]
        o_ref[:, p * LANES:(p + 1) * LANES] = jnp.where(lane < SWA_HEAD_DIM, o_a, o_b).astype(BF16)


def _window_attention(att, sink, n_batch, n_lat, n_ctx, with_ctx):
    T = att.shape[0]
    group = SWA_Q_HEADS // SWA_KV_HEADS
    qw = group * SWA_HEAD_DIM
    n_lat_blocks = n_lat // SWA_BLOCK
    n_ctx_blocks = n_ctx // SWA_BLOCK if with_ctx else 0
    k_col = SWA_Q_HEADS * SWA_HEAD_DIM // LANES
    v_col = k_col + SWA_KV_HEADS
    ctx_row = n_batch * n_lat // n_ctx

    def q_map(b, kvh, nb):
        lat = b * n_lat_blocks + nb
        ctx = n_batch * n_lat_blocks + b * (n_ctx // SWA_BLOCK) + (nb - n_lat_blocks)
        return (jnp.where(nb < n_lat_blocks, lat, ctx), kvh)

    def cls_map(b, kvh, nb):
        cls = jnp.where(nb == 0, 0, jnp.where(nb == n_lat_blocks - 1, 2, jnp.where(nb >= n_lat_blocks, 3, 1)))
        return (cls, 0, 0)

    grid_spec = pltpu.PrefetchScalarGridSpec(
        num_scalar_prefetch=0,
        grid=(n_batch, SWA_KV_HEADS, n_lat_blocks + n_ctx_blocks),
        in_specs=[pl.BlockSpec(memory_space=pltpu.SMEM),
                  pl.BlockSpec((SWA_BLOCK, qw), q_map),
                  pl.BlockSpec((n_lat, LANES), lambda b, kvh, nb: (b, k_col + kvh)),
                  pl.BlockSpec((n_lat, LANES), lambda b, kvh, nb: (b, v_col + kvh)),
                  pl.BlockSpec((n_ctx, LANES), lambda b, kvh, nb: (ctx_row + b, k_col + kvh)),
                  pl.BlockSpec((n_ctx, LANES), lambda b, kvh, nb: (ctx_row + b, v_col + kvh)),
                  pl.BlockSpec((1, SWA_BLOCK, SWA_BAND), cls_map)],
        out_specs=pl.BlockSpec((SWA_BLOCK, qw), q_map),
        scratch_shapes=[pltpu.VMEM((group * SWA_BLOCK, SWA_BAND + n_ctx), F32),
                        pltpu.VMEM((group * SWA_BLOCK, SWA_BAND + n_ctx), BF16),
                        pltpu.VMEM((group * SWA_BLOCK, LANES), F32)],
    )
    return pl.pallas_call(
        functools.partial(_swa_kernel, n_lat=n_lat, group=group),
        grid_spec=grid_spec,
        out_shape=jax.ShapeDtypeStruct((T if with_ctx else n_batch * n_lat, SWA_Q_HEADS * SWA_HEAD_DIM), BF16),
        compiler_params=_params(("arbitrary", "arbitrary", "arbitrary")), name="window_attention",
    )(sink.astype(F32), att, att, att, att, att, _swa_mask_table(n_lat))


DN_PAIR = 2 * DN_CHUNK
DN_GROUP = 4


def _bdot(a, b):
    return jnp.einsum('gik,gkj->gij', a, b, preferred_element_type=F32)


def _bdot_nt(a, b):
    return jnp.einsum('gik,gjk->gij', a, b, preferred_element_type=F32)


def _split2(x):
    hi = x.astype(BF16)
    return hi, (x - hi.astype(F32)).astype(BF16)


def _bdot3(a, b):
    ah, al = _split2(a)
    bh, bl = _split2(b)
    return _bdot(ah, bh) + (_bdot(ah, bl) + _bdot(al, bh))


def _softplus(x):
    return jnp.maximum(x, 0.0) + jnp.log(1.0 + jnp.exp(-jnp.abs(x)))


def _dn_prepare_direction(q, k, v, gc, beta, fwd):
    G = q.shape[0]
    ii = lax.broadcasted_iota(jnp.int32, (DN_PAIR, DN_PAIR), 0)
    jj = lax.broadcasted_iota(jnp.int32, (DN_PAIR, DN_PAIR), 1)
    same = (ii >> 6) == (jj >> 6)
    incl = same & ((ii >= jj) if fwd else (ii <= jj))
    strict = same & ((ii > jj) if fwd else (ii < jj))
    eye = (ii == jj).astype(F32)
    lane0 = jnp.broadcast_to((jj == 0).astype(BF16), (G, DN_PAIR, DN_PAIR))
    g_row = sum(_bdot_nt(lane0, p) for p in _split3(gc))
    decay = jnp.where(incl, jnp.exp(jnp.where(incl, gc - g_row, 0.0)), 0.0)
    kb = k * beta
    vb = v * beta
    k16 = k.astype(BF16)
    m = jnp.where(strict, _bdot_nt(kb.astype(BF16), k16) * decay, 0.0)
    inv = eye - m
    p = m
    for _ in range(5):
        p = _bdot3(p, p)
        inv = inv + _bdot3(inv, p)
    inv16 = inv.astype(BF16)
    eg = jnp.exp(gc)
    u = _bdot(inv16, vb.astype(BF16))
    w = _bdot(inv16, (kb * eg).astype(BF16))
    a_intra = jnp.where(incl, _bdot_nt(q.astype(BF16), k16) * decay, 0.0)
    lo, hi = (DN_CHUNK - 1, DN_PAIR - 1) if fwd else (0, DN_CHUNK)
    first_chunk = lax.broadcasted_iota(jnp.int32, (DN_PAIR, 1), 0) < DN_CHUNK
    g_last = jnp.where(first_chunk, gc[:, lo:lo + 1, :], gc[:, hi:hi + 1, :])
    kg = k * jnp.exp(g_last - gc)
    eye16 = jnp.broadcast_to(eye.astype(BF16), (G, DN_PAIR, DN_PAIR))
    kg_t = _bdot_nt(eye16, kg.astype(BF16))
    return u, w.astype(BF16), a_intra.astype(BF16), (q * eg).astype(BF16), kg_t.astype(BF16), jnp.exp(g_last)


def _dn_sequence(q_ref, k_ref, v_ref, z_ref, ab_ref, cw_refs, gpar_ref, nw_ref, o_ref, s0, head, T, sc):
    (qn, kn, vn, gcs, bts, us, ws, aas, qgs, kgts, egls, outs) = sc
    row = lax.broadcasted_iota(jnp.int32, (T, 1), 0)

    def conv_silu(x, w_ref):
        acc = x * w_ref[DN_CONV // 2:DN_CONV // 2 + 1, :]
        for j in range(DN_CONV):
            d = j - DN_CONV // 2
            if d == 0:
                continue
            shifted = pltpu.roll(x, (-d) % T, axis=0)
            ok = (row + d >= 0) & (row + d < T)
            acc = acc + jnp.where(ok, shifted, 0.0) * w_ref[j:j + 1, :]
        return acc * _sigmoid(acc)

    q = conv_silu(q_ref[...], cw_refs[0])
    k = conv_silu(k_ref[...], cw_refs[1])
    v = conv_silu(v_ref[...], cw_refs[2])
    q = q * lax.rsqrt(jnp.sum(q * q, axis=1, keepdims=True) + RMS_EPS) * DN_HEAD_DIM ** -0.5
    k = k * lax.rsqrt(jnp.sum(k * k, axis=1, keepdims=True) + RMS_EPS)
    qn[pl.ds(0, T), :] = q
    kn[pl.ds(0, T), :] = k
    vn[pl.ds(0, T), :] = v

    ab = ab_ref[...]
    g_all = -jnp.exp(gpar_ref[0:1, :]) * _softplus(ab + gpar_ref[1:2, :])
    b_all = _sigmoid(ab)
    col = lax.broadcasted_iota(jnp.int32, (LANES, LANES), 0)

    def column(x, c):
        pick = (col == c).astype(BF16)
        return sum(_dot(p, pick) for p in _split3(x))

    pos = row & (DN_CHUNK - 1)
    for d in range(2):
        g = column(g_all, d * DN_HEADS + head)
        for sh in (1, 2, 4, 8, 16, 32):
            if d == 0:
                g = g + jnp.where(pos >= sh, pltpu.roll(g, sh, axis=0), 0.0)
            else:
                g = g + jnp.where(pos < DN_CHUNK - sh, pltpu.roll(g, T - sh, axis=0), 0.0)
        gcs[d][pl.ds(0, T), :] = g
        bts[d][pl.ds(0, T), :] = column(b_all, 2 * DN_HEADS + d * DN_HEADS + head)

    n_pairs = T // DN_PAIR
    G = min(DN_GROUP, n_pairs)
    R = G * DN_PAIR

    def prepare(gi, carry):
        rows = pl.ds(pl.multiple_of(gi * R, R), R)
        shape = (G, DN_PAIR, LANES)
        qq = qn[rows, :].reshape(shape)
        kk = kn[rows, :].reshape(shape)
        vv = vn[rows, :].reshape(shape)
        for d in range(2):
            res = _dn_prepare_direction(qq, kk, vv, gcs[d][rows, :].reshape(shape), bts[d][rows, :].reshape(shape), d == 0)
            for ref, val in zip((us[d], ws[d], aas[d], qgs[d], kgts[d], egls[d]), res):
                ref[rows, :] = val.reshape(R, LANES)
        return carry

    lax.fori_loop(0, n_pairs // G, prepare, 0)

    zeros = jnp.zeros((DN_CHUNK, LANES), BF16)

    def chunk_step(S, base, half, d):
        off = pl.multiple_of(base + half * DN_CHUNK, DN_CHUNK)
        rows = pl.ds(off, DN_CHUNK)
        s16 = S.astype(BF16)
        v_new = us[d][rows, :] - _dot(ws[d][rows, :], s16)
        v16 = v_new.astype(BF16)
        v2 = jnp.concatenate([v16, zeros] if half == 0 else [zeros, v16], axis=0)
        outs[d][rows, :] = _dot(qgs[d][rows, :], s16) + _dot(aas[d][rows, :], v2)
        decay = jnp.broadcast_to(egls[d][pl.ds(off, 1), :], (DN_PAIR, LANES))
        return S * decay + _dot(kgts[d][pl.ds(pl.multiple_of(base, DN_PAIR), DN_PAIR), :], v2)

    def scan(p, carry):
        s_f, s_b = carry
        base_f = p * DN_PAIR
        base_b = (n_pairs - 1 - p) * DN_PAIR
        s_f = chunk_step(s_f, base_f, 0, 0)
        s_b = chunk_step(s_b, base_b, 1, 1)
        s_f = chunk_step(s_f, base_f, 1, 0)
        s_b = chunk_step(s_b, base_b, 0, 1)
        return s_f, s_b

    s_f, s_b = lax.fori_loop(0, n_pairs, scan, s0)

    o = outs[0][pl.ds(0, T), :] + outs[1][pl.ds(0, T), :]
    o = o * lax.rsqrt(jnp.mean(o * o, axis=1, keepdims=True) + RMS_EPS) * nw_ref[...]
    z = z_ref[...]
    o_ref[...] = (o * (z * _sigmoid(z))).astype(BF16)
    return s_f, s_b


def _dn_kernel(ql, kl, vl, zl, abl, qc, kc, vc, zc, abc, cwq, cwk, cwv, gpar, nw, ol, oc, *scratch, n_lat, n_ctx):
    head = pl.program_id(1)
    names = 12
    sc = []
    it = iter(scratch)
    for i in range(names):
        sc.append(next(it) if i < 3 else (next(it), next(it)))
    zero = jnp.zeros((DN_HEAD_DIM, DN_HEAD_DIM), F32)
    s_ctx = _dn_sequence(qc, kc, vc, zc, abc, (cwq, cwk, cwv), gpar, nw, oc, (zero, zero), head, n_ctx, sc)
    _dn_sequence(ql, kl, vl, zl, abl, (cwq, cwk, cwv), gpar, nw, ol, s_ctx, head, n_lat, sc)


def _delta_net(dn, conv_w, a_log, dt_bias, norm_w, n_batch, n_lat, n_ctx):
    H = DN_HEADS
    ctx_row = n_batch * n_lat // n_ctx
    ab_col = 4 * H

    def lat(off):
        return pl.BlockSpec((n_lat, LANES), lambda b, h: (b, off + h))

    def ctx(off):
        return pl.BlockSpec((n_ctx, LANES), lambda b, h: (ctx_row + b, off + h))

    cw = jnp.zeros((SUBLANES, 3 * DN_WIDTH), F32).at[:DN_CONV].set(conv_w.astype(F32))
    gpar = jnp.zeros((SUBLANES, LANES), F32)
    gpar = gpar.at[0, :2 * H].set(a_log.reshape(-1).astype(F32)).at[1, :2 * H].set(dt_bias.reshape(-1).astype(F32))
    in_specs = [lat(0), lat(H), lat(2 * H), lat(3 * H), pl.BlockSpec((n_lat, LANES), lambda b, h: (b, ab_col)),
                ctx(0), ctx(H), ctx(2 * H), ctx(3 * H), pl.BlockSpec((n_ctx, LANES), lambda b, h: (ctx_row + b, ab_col)),
                pl.BlockSpec((SUBLANES, LANES), lambda b, h: (0, h)),
                pl.BlockSpec((SUBLANES, LANES), lambda b, h: (0, H + h)),
                pl.BlockSpec((SUBLANES, LANES), lambda b, h: (0, 2 * H + h)),
                pl.BlockSpec((SUBLANES, LANES), lambda b, h: (0, 0)),
                pl.BlockSpec((1, LANES), lambda b, h: (0, 0))]
    f32s = lambda: pltpu.VMEM((n_lat, LANES), F32)
    b16s = lambda: pltpu.VMEM((n_lat, LANES), BF16)
    scratch = [f32s(), f32s(), f32s()]
    scratch += [f32s() for _ in range(4)]
    scratch += [f32s(), f32s()]
    scratch += [b16s() for _ in range(8)]
    scratch += [f32s() for _ in range(4)]
    return pl.pallas_call(
        functools.partial(_dn_kernel, n_lat=n_lat, n_ctx=n_ctx),
        grid=(n_batch, H),
        in_specs=in_specs,
        out_specs=[pl.BlockSpec((n_lat, LANES), lambda b, h: (b, h)),
                   pl.BlockSpec((n_ctx, LANES), lambda b, h: (b, h))],
        out_shape=[jax.ShapeDtypeStruct((n_batch * n_lat, DN_WIDTH), BF16),
                   jax.ShapeDtypeStruct((n_batch * n_ctx, DN_WIDTH), BF16)],
        scratch_shapes=scratch,
        compiler_params=_params(("arbitrary", "arbitrary")), name="delta_net",
    )(dn, dn, dn, dn, dn, dn, dn, dn, dn, dn, cw, cw, cw, gpar, norm_w.reshape(1, -1).astype(F32))


def _rope_tables(n_tokens):
    t = jnp.arange(n_tokens)
    n_freq = SWA_HEAD_DIM // 4
    inv = ROPE_THETA ** (-jnp.arange(n_freq, dtype=F32) / n_freq)
    row = (t // GRID_W).astype(F32)[:, None]
    col = (t % GRID_W).astype(F32)[:, None]
    ang = jnp.concatenate([row * inv, col * inv], -1)
    cos, sin = jnp.cos(ang), jnp.sin(ang)
    cos_t = jnp.concatenate([cos, cos, cos, cos], -1)
    sin_t = jnp.concatenate([-sin, -sin, sin, sin], -1)
    ident = jnp.ones((TOK_TILE, LANES), F32)
    return (jnp.concatenate([cos_t, ident], 0), jnp.concatenate([sin_t, jnp.zeros_like(ident)], 0))


def _swa_weight(w_in):
    D = w_in.shape[0]
    qw = SWA_Q_HEADS * SWA_HEAD_DIM
    kw = SWA_KV_HEADS * SWA_HEAD_DIM
    half = SWA_HEAD_DIM // 2
    wq = w_in[:, :qw].reshape(D, SWA_Q_HEADS // 2, 2, 2, half)
    wq = jnp.transpose(wq, (0, 1, 3, 2, 4)).reshape(D, qw)
    wk = w_in[:, qw:qw + kw].reshape(D, SWA_KV_HEADS, 2, 1, half)
    wk = jnp.broadcast_to(wk, (D, SWA_KV_HEADS, 2, 2, half)).reshape(D, 2 * kw)
    wv = w_in[:, qw + kw:].reshape(D, SWA_KV_HEADS, 1, SWA_HEAD_DIM)
    wv = jnp.broadcast_to(wv, (D, SWA_KV_HEADS, 2, SWA_HEAD_DIM)).reshape(D, 2 * kw)
    return jnp.concatenate([wq, wk, wv], axis=1).astype(BF16)


def kernel(x, c, ctx, c_ctx, w_mod, b_mod, ln_g, ln_b, w_in_ab, na_rpb, dn_conv, dn_a_log, dn_dt_bias, dn_norm_w,
           w_out_ab, w_in_c, swa_sink, w_out_c, w_router, b_router, w_gu, b_gu, w_down, b_down):
    B, N, D = x.shape
    L = ctx.shape[1]
    depth = w_mod.shape[0]
    alpha = (2 * depth) ** 0.25
    assert D == D_MODEL and N % TOK_TILE == 0 and L == TOK_TILE
    geom = (B * N // TOK_TILE, N // TOK_TILE, B)

    mod_rows = -(-(B + 1) // 16) * 16
    cc = jnp.zeros((mod_rows, D), F32).at[:B].set(c).at[B].set(c_ctx)
    mod_all = _modulation(cc, w_mod, b_mod)
    cos_t, sin_t = _rope_tables(N)
    n_att_ab = 3 * NA_WIDTH
    ab_main = n_att_ab + 4 * DN_WIDTH
    wr = jnp.zeros((depth, D, LANES), BF16).at[:, :, :N_EXPERTS].set(w_router.astype(BF16))
    br = jnp.full((depth, 1, LANES), NEG_INF, F32).at[:, 0, :N_EXPERTS].set(b_router.astype(F32))
    wgu16, wd16 = w_gu.astype(BF16), w_down.astype(BF16)
    bgu4, bd4 = b_gu[:, :, None, :].astype(F32), b_down[:, :, None, :].astype(F32)

    h = jnp.concatenate([x.reshape(B * N, D), ctx.reshape(B * L, D)], axis=0)
    moe_prev = mod_prev = ln_prev = None
    out = None
    for layer in range(depth):
        last = layer == depth - 1
        i = layer // 2
        mod = mod_all[layer].reshape(mod_rows, 1, -1)
        if layer % 2 == 0:
            w = w_in_ab[i]
            w = jnp.concatenate([w[:, :ab_main], jnp.pad(w[:, ab_main:], ((0, 0), (0, LANES - (w.shape[1] - ab_main))))], 1)
            h, att, dn = _proj_in(h, moe_prev, mod_prev, ln_prev, mod, w.astype(BF16), geom, n_att=n_att_ab, alpha=alpha)
            o_a = _neighbourhood_attention(att, _na_bias_table(na_rpb[i], N // GRID_W), B, N, L)
            o_bl, o_bc = _delta_net(dn, dn_conv[i], dn_a_log[i], dn_dt_bias[i], dn_norm_w[i], B, N, L)
            xs = [o_a, jnp.concatenate([o_bl, o_bc], axis=0)]
            wo = w_out_ab[i].astype(BF16)
            ws = [wo[:NA_WIDTH], wo[NA_WIDTH:]]
        else:
            w = _swa_weight(w_in_c[i])
            n_rope = (SWA_Q_HEADS + 2 * SWA_KV_HEADS) * SWA_HEAD_DIM // LANES
            h, att, _ = _proj_in(h, moe_prev, mod_prev, ln_prev, mod, w, geom, n_att=w.shape[1],
                                 rope=(cos_t, sin_t, n_rope), alpha=alpha)
            xs = [_window_attention(att, swa_sink[i], B, N, L, with_ctx=not last)]
            ws = [w_out_c[i].astype(BF16)]
        n_rows = B * N if last else B * (N + L)
        ln1 = jnp.stack([ln_g[layer, 0], ln_b[layer, 0]])
        ln2 = jnp.stack([ln_g[layer, 1], ln_b[layer, 1]])
        h1, f_moe, idx, gates = _proj_out(xs, ws, h, mod, ln1, wr[layer], br[layer], geom, n_rows, alpha=alpha)
        y_moe = _moe(f_moe, idx, gates, wgu16, bgu4, wd16, bd4, layer)
        if last:
            out = _final_combine(h1, y_moe, mod, ln2, geom, alpha=alpha).reshape(B, N, D)
        else:
            h, moe_prev, mod_prev, ln_prev = h1, y_moe, mod, ln2
    return out
```

```python
import functools

import jax
import jax.numpy as jnp
import numpy as np
from jax import lax
from jax.experimental import pallas as pl
from jax.experimental.pallas import tpu as pltpu

F32 = jnp.float32
BF16 = jnp.bfloat16

D_MODEL = 1024
GRID_W = 64
NA_HEADS = 8
NA_HEAD_DIM = 64
NA_WIN_H = 8
NA_WIN_W = 16
DN_HEADS = 4
DN_HEAD_DIM = 128
DN_CONV = 5
DN_CHUNK = 64
SWA_Q_HEADS = 16
SWA_KV_HEADS = 2
SWA_HEAD_DIM = 64
SWA_WINDOW = 128
ROPE_THETA = 10000.0
N_EXPERTS = 32
TOP_K = 4
D_EXPERT = 1024
SWIGLU_LIMIT = 7.0
SWIGLU_ALPHA = 1.702
LN_EPS = 1e-5
RMS_EPS = 1e-6
NEG_INF = -1e30

LANES = 128
SUBLANES = 8
TOK_TILE = 256
MOE_TILE = 128
MOE_CHUNK_MAX = 3072
VMEM_LIMIT = 56 * 1024 * 1024

NA_WIDTH = NA_HEADS * NA_HEAD_DIM
DN_WIDTH = DN_HEADS * DN_HEAD_DIM
NA_QROWS = 4
NA_KROWS = 12


def _params(sem):
    return pltpu.CompilerParams(dimension_semantics=sem, vmem_limit_bytes=VMEM_LIMIT)


def _sigmoid(x):
    return 1.0 / (1.0 + jnp.exp(-x))


def _dot(a, b):
    return jnp.dot(a, b, preferred_element_type=F32)


def _dot_nt(a, b):
    return lax.dot_general(a, b, (((1,), (1,)), ((), ())), preferred_element_type=F32)


def _moe_layout_load(ref, rows):
    return jnp.concatenate([ref[pl.ds(s, rows, stride=SUBLANES), :] for s in range(SUBLANES)], axis=1)


def _moe_layout_store(ref, val, rows):
    for s in range(SUBLANES):
        ref[pl.ds(s, rows, stride=SUBLANES), :] = val[:, s * LANES:(s + 1) * LANES]


def _layer_norm(y, g, b):
    mu = jnp.mean(y, axis=-1, keepdims=True)
    d = y - mu
    var = jnp.mean(d * d, axis=-1, keepdims=True)
    return d * lax.rsqrt(var + LN_EPS) * g + b


def _mod_kernel(s_ref, w_ref, b_ref, o_ref):
    s = s_ref[...]
    s = s * _sigmoid(s)
    o_ref[0] = _dot(s.astype(BF16), w_ref[0].astype(BF16)) + b_ref[0]


def _modulation(cc, w_mod, b_mod):
    depth = w_mod.shape[0]
    rows = cc.shape[0]
    nblk = w_mod.shape[2] // D_MODEL
    return pl.pallas_call(
        _mod_kernel,
        grid=(depth, nblk),
        in_specs=[pl.BlockSpec((rows, D_MODEL), lambda l, j: (0, 0)),
                  pl.BlockSpec((1, D_MODEL, D_MODEL), lambda l, j: (l, 0, j)),
                  pl.BlockSpec((1, 1, D_MODEL), lambda l, j: (l, 0, j))],
        out_specs=pl.BlockSpec((1, rows, D_MODEL), lambda l, j: (l, 0, j)),
        out_shape=jax.ShapeDtypeStruct((depth, rows, nblk * D_MODEL), F32),
        compiler_params=_params(("arbitrary", "arbitrary")),
        name="modulation",
    )(cc, w_mod, b_mod.reshape(depth, 1, -1))


def _pin_kernel(*refs, first, n_rope, n_att, alpha):
    refs = list(refs)
    h_ref = refs.pop(0)
    if not first:
        moe_ref, g2_ref, lg_ref, lb_ref = refs[:4]
        refs = refs[4:]
    sc_ref, sh_ref, w_ref = refs[:3]
    refs = refs[3:]
    if n_rope:
        cos_ref, sin_ref = refs[:2]
        refs = refs[2:]
    if not first:
        hout_ref = refs.pop(0)
    att_ref = refs.pop(0)
    dn_ref = refs.pop(0) if refs else None

    h = h_ref[...]
    if not first:
        m = _moe_layout_load(moe_ref, TOK_TILE)
        h = _layer_norm(alpha * h + g2_ref[0] * m, lg_ref[...], lb_ref[...])
        hout_ref[...] = h
    a = h * (1.0 + sc_ref[0]) + sh_ref[0]
    acc = _dot(a.astype(BF16), w_ref[...])
    if n_rope:
        cos = cos_ref[...]
        sin = sin_ref[...]
        for g in range(n_att // LANES):
            blk = acc[:, g * LANES:(g + 1) * LANES]
            if g < n_rope:
                blk = blk * cos + pltpu.roll(blk, LANES // 2, axis=1) * sin
            att_ref[:, g * LANES:(g + 1) * LANES] = blk.astype(BF16)
    else:
        att_ref[...] = acc[:, :n_att].astype(BF16)
    if dn_ref is not None:
        dn_ref[...] = acc[:, n_att:]


def _mod_spec(k, n_lat_tiles, lat_tiles_per_batch, n_batch):
    def imap(i):
        return (jnp.where(i < n_lat_tiles, i // lat_tiles_per_batch, n_batch), 0, k)
    return pl.BlockSpec((1, 1, D_MODEL), imap)


def _proj_in(h, moe_prev, mod_prev, ln_prev, mod, w, geom, *, n_att, rope=None, alpha=1.0):
    T = h.shape[0]
    n_tiles = T // TOK_TILE
    n_lat_tiles, per_batch, n_batch = geom
    first = moe_prev is None
    ncols = w.shape[1]
    row = pl.BlockSpec((TOK_TILE, D_MODEL), lambda i: (i, 0))
    vec = pl.BlockSpec((1, D_MODEL), lambda i: (0, 0))
    ms = functools.partial(_mod_spec, n_lat_tiles=n_lat_tiles, lat_tiles_per_batch=per_batch, n_batch=n_batch)
    args, specs = [h], [row]
    if not first:
        moe_arr, moe_chunk = moe_prev
        args += [moe_arr, mod_prev, ln_prev[0:1], ln_prev[1:2]]
        specs += [pl.BlockSpec((TOK_TILE * SUBLANES, LANES), lambda i: (_moe_block(i, moe_chunk), 0)), ms(5), vec, vec]
    args += [mod, mod, w]
    specs += [ms(1), ms(0), pl.BlockSpec((D_MODEL, ncols), lambda i: (0, 0))]
    n_rope = 0
    if rope is not None:
        cos_t, sin_t, n_rope = rope
        tab = pl.BlockSpec((TOK_TILE, LANES), lambda i: (jnp.where(i < n_lat_tiles, i % per_batch, per_batch), 0))
        args += [cos_t, sin_t]
        specs += [tab, tab]
    out_shapes, out_specs = [], []
    if not first:
        out_shapes.append(jax.ShapeDtypeStruct((T, D_MODEL), F32))
        out_specs.append(row)
    out_shapes.append(jax.ShapeDtypeStruct((T, n_att), BF16))
    out_specs.append(pl.BlockSpec((TOK_TILE, n_att), lambda i: (i, 0)))
    if ncols > n_att:
        out_shapes.append(jax.ShapeDtypeStruct((T, ncols - n_att), F32))
        out_specs.append(pl.BlockSpec((TOK_TILE, ncols - n_att), lambda i: (i, 0)))
    outs = pl.pallas_call(
        functools.partial(_pin_kernel, first=first, n_rope=n_rope, n_att=n_att, alpha=alpha),
        grid=(n_tiles,), in_specs=specs, out_specs=out_specs, out_shape=out_shapes,
        compiler_params=_params(("arbitrary",)), name="proj_in",
    )(*args)
    outs = list(outs)
    h_new = h if first else outs.pop(0)
    att = outs.pop(0)
    dn = outs.pop(0) if outs else None
    return h_new, att, dn


def _final_kernel(h_ref, moe_ref, g2_ref, lg_ref, lb_ref, o_ref, *, alpha):
    m = _moe_layout_load(moe_ref, TOK_TILE)
    o_ref[...] = _layer_norm(alpha * h_ref[...] + g2_ref[0] * m, lg_ref[...], lb_ref[...])


def _final_combine(h, moe, mod, ln, geom, *, alpha):
    T = h.shape[0]
    n_lat_tiles, per_batch, n_batch = geom
    moe, moe_chunk = moe
    row = pl.BlockSpec((TOK_TILE, D_MODEL), lambda i: (i, 0))
    vec = pl.BlockSpec((1, D_MODEL), lambda i: (0, 0))
    return pl.pallas_call(
        functools.partial(_final_kernel, alpha=alpha),
        grid=(T // TOK_TILE,),
        in_specs=[row, pl.BlockSpec((TOK_TILE * SUBLANES, LANES), lambda i: (_moe_block(i, moe_chunk), 0)),
                  _mod_spec(5, n_lat_tiles, per_batch, n_batch), vec, vec],
        out_specs=row, out_shape=jax.ShapeDtypeStruct((T, D_MODEL), F32),
        compiler_params=_params(("arbitrary",)), name="final_combine",
    )(h, moe, mod, ln[0:1], ln[1:2])


def _pout_kernel(*refs, n_in, alpha):
    xs = refs[:n_in]
    ws = refs[n_in:2 * n_in]
    (h_ref, g1_ref, lg_ref, lb_ref, sc_ref, sh_ref, wr_ref, br_ref,
     hout_ref, f_ref, idx_ref, gate_ref) = refs[2 * n_in:]
    m = _dot(xs[0][...], ws[0][...])
    for x_ref, w_ref in zip(xs[1:], ws[1:]):
        m = m + _dot(x_ref[...], w_ref[...])
    h1 = _layer_norm(alpha * h_ref[...] + g1_ref[0] * m, lg_ref[...], lb_ref[...])
    hout_ref[...] = h1
    f = h1 * (1.0 + sc_ref[0]) + sh_ref[0]
    _moe_layout_store(f_ref, f, TOK_TILE)
    logits = _dot(f.astype(BF16), wr_ref[...]) + br_ref[...]
    lane = lax.broadcasted_iota(jnp.int32, logits.shape, 1)
    work = logits
    vals, idxs = [], []
    for _ in range(TOP_K):
        v = jnp.max(work, axis=1, keepdims=True)
        ix = jnp.min(jnp.where(work == v, lane, LANES), axis=1, keepdims=True)
        vals.append(v)
        idxs.append(ix)
        work = jnp.where(lane == ix, -jnp.inf, work)
    es = [jnp.exp(v - vals[0]) for v in vals]
    den = es[0]
    for e in es[1:]:
        den = den + e
    idx_out = jnp.zeros(logits.shape, jnp.int32)
    gate_out = jnp.zeros(logits.shape, F32)
    for k in range(TOP_K):
        idx_out = jnp.where(lane == k, idxs[k], idx_out)
        gate_out = jnp.where(lane == k, es[k] / den, gate_out)
    idx_ref[...] = idx_out
    gate_ref[...] = gate_out


def _proj_out(xs, ws, h, mod, ln, w_router, b_router, geom, n_rows, *, alpha):
    n_lat_tiles, per_batch, n_batch = geom
    row = pl.BlockSpec((TOK_TILE, D_MODEL), lambda i: (i, 0))
    vec = pl.BlockSpec((1, D_MODEL), lambda i: (0, 0))
    lane_row = pl.BlockSpec((TOK_TILE, LANES), lambda i: (i, 0))
    ms = functools.partial(_mod_spec, n_lat_tiles=n_lat_tiles, lat_tiles_per_batch=per_batch, n_batch=n_batch)
    specs = [pl.BlockSpec((TOK_TILE, x.shape[1]), lambda i: (i, 0)) for x in xs]
    specs += [pl.BlockSpec(w.shape, lambda i: (0, 0)) for w in ws]
    specs += [row, ms(2), vec, vec, ms(4), ms(3),
              pl.BlockSpec((D_MODEL, LANES), lambda i: (0, 0)), pl.BlockSpec((1, LANES), lambda i: (0, 0))]
    return pl.pallas_call(
        functools.partial(_pout_kernel, n_in=len(xs), alpha=alpha),
        grid=(n_rows // TOK_TILE,), in_specs=specs,
        out_specs=[row, pl.BlockSpec((TOK_TILE * SUBLANES, LANES), lambda i: (i, 0)), lane_row, lane_row],
        out_shape=[jax.ShapeDtypeStruct((n_rows, D_MODEL), F32),
                   jax.ShapeDtypeStruct((n_rows * SUBLANES, LANES), F32),
                   jax.ShapeDtypeStruct((n_rows, LANES), jnp.int32),
                   jax.ShapeDtypeStruct((n_rows, LANES), F32)],
        compiler_params=_params(("arbitrary",)), name="proj_out",
    )(*xs, *ws, h, mod, ln[0:1], ln[1:2], mod, mod, w_router, b_router)


def _split3(x):
    hi = x.astype(BF16)
    r = x - hi.astype(F32)
    mid = r.astype(BF16)
    lo = (r - mid.astype(F32)).astype(BF16)
    return hi, mid, lo


MOE_UNROLL = 8


MOE_TRASH = TOK_TILE
TILE_ROWS = MOE_TILE * SUBLANES


def _moe_kernel(goff_ref, cnt_ref, tok_hbm, gw_hbm, f_ref, wgu_ref, bgu_ref, wd_ref, bd_ref,
                o_ref, xin_ref, y_ref, tok_s, gw_s, sem, *, n_slots, chunk):
    c = pl.program_id(0)
    e = pl.program_id(1)

    @pl.when(e == 0)
    def _():
        src = pl.ds(pl.multiple_of(c * n_slots, 1024), n_slots)
        cp_tok = pltpu.make_async_copy(tok_hbm.at[src], tok_s.at[pl.ds(0, n_slots)], sem.at[0])
        cp_gw = pltpu.make_async_copy(gw_hbm.at[src], gw_s.at[pl.ds(0, n_slots)], sem.at[1])
        cp_tok.start()
        cp_gw.start()
        o_ref[...] = jnp.zeros(o_ref.shape, F32)
        cp_tok.wait()
        cp_gw.wait()

    @pl.when((c == 0) & (e == 0))
    def _():
        xin_ref[...] = jnp.zeros(xin_ref.shape, F32)
        y_ref[...] = jnp.zeros(y_ref.shape, F32)
        for j in range(MOE_TILE):
            tok_s[n_slots + j] = 0
            gw_s[n_slots + j] = 0.0

    g = c * N_EXPERTS + e
    group0 = goff_ref[g]
    n_rows = cnt_ref[g]
    n_tiles = (n_rows + MOE_TILE - 1) // MOE_TILE

    def rows8(i):
        return pl.ds(pl.multiple_of(i * SUBLANES, SUBLANES), SUBLANES)

    def gather(parity, off):
        for j in range(MOE_TILE):
            t = jnp.minimum(tok_s[off + j], chunk - 1)
            xin_ref[rows8(parity * MOE_TILE + j), :] = f_ref[rows8(t), :]

    def scatter(parity, off, live):
        for j0 in range(0, MOE_TILE, MOE_UNROLL):
            ts, gs = [], []
            for j in range(j0, j0 + MOE_UNROLL):
                t, gwt = tok_s[off + j], gw_s[off + j]
                if live is not None:
                    t = jnp.where(live, t, chunk + j)
                    gwt = jnp.where(live, gwt, 0.0)
                ts.append(rows8(t))
                gs.append(gwt)
            olds = [o_ref[d, :] for d in ts]
            for u, (d, gwt, old) in enumerate(zip(ts, gs, olds)):
                o_ref[d, :] = old + gwt * y_ref[rows8(parity * MOE_TILE + j0 + u), :]

    @pl.when(n_rows > 0)
    def _():
        gather(0, group0)

        def tile(r, carry):
            parity = r & 1
            off = group0 + r * MOE_TILE
            gather(1 - parity, off + MOE_TILE)
            base = parity * TILE_ROWS
            x = jnp.concatenate([xin_ref[pl.ds(base + s, MOE_TILE, stride=SUBLANES), :] for s in range(SUBLANES)],
                                axis=1).astype(BF16)
            gu = _dot(x, wgu_ref[0, 0]) + bgu_ref[0, 0]
            gate = jnp.minimum(gu[:, :D_EXPERT], SWIGLU_LIMIT)
            up = jnp.clip(gu[:, D_EXPERT:], -SWIGLU_LIMIT, SWIGLU_LIMIT)
            hid = gate * _sigmoid(SWIGLU_ALPHA * gate) * (up + 1.0)
            y = _dot(hid.astype(BF16), wd_ref[0, 0]) + bd_ref[0, 0]
            scatter(1 - parity, jnp.maximum(off - MOE_TILE, 0), r > 0)
            for s in range(SUBLANES):
                y_ref[pl.ds(base + s, MOE_TILE, stride=SUBLANES), :] = y[:, s * LANES:(s + 1) * LANES]
            return carry

        lax.fori_loop(0, n_tiles, tile, 0)
        last = n_tiles - 1
        scatter(last & 1, group0 + last * MOE_TILE, None)


def _moe_chunk(n_rows):
    c = MOE_CHUNK_MAX
    while n_rows % c:
        c -= TOK_TILE
    return c


def _moe_block(i, chunk):
    per = chunk // TOK_TILE
    return (i // per) * (per + MOE_TRASH // TOK_TILE) + i % per


def _moe_metadata(idx, gates, chunk):
    n_rows = idx.shape[0]
    n_chunks = n_rows // chunk
    n_assign = chunk * TOP_K
    n_pad = N_EXPERTS * MOE_TILE
    e = idx[:, :TOP_K].reshape(n_chunks, n_assign)
    gv = gates[:, :TOP_K].reshape(n_chunks, n_assign)
    tokv = jnp.broadcast_to(jnp.arange(n_assign, dtype=jnp.int32) // TOP_K, (n_chunks, n_assign))
    pad_e = jnp.arange(n_pad, dtype=jnp.int32) // MOE_TILE
    pad_t = chunk + jnp.arange(n_pad, dtype=jnp.int32) % MOE_TILE
    keys = jnp.concatenate([2 * e, jnp.broadcast_to(2 * pad_e + 1, (n_chunks, n_pad))], axis=1)
    toks = jnp.concatenate([tokv, jnp.broadcast_to(pad_t, (n_chunks, n_pad))], axis=1)
    gws = jnp.concatenate([gv, jnp.zeros((n_chunks, n_pad), F32)], axis=1)
    _, tok, gw = lax.sort((keys, toks, gws), dimension=1, num_keys=1, is_stable=True)
    cnt = jnp.sum((e[:, :, None] == jnp.arange(N_EXPERTS, dtype=jnp.int32)).astype(jnp.int32), axis=1)
    goff = jnp.cumsum(cnt + MOE_TILE, axis=1) - (cnt + MOE_TILE)
    return tok.reshape(-1), gw.reshape(-1), goff.reshape(-1), cnt.reshape(-1), n_assign + n_pad


def _moe(f_moe, idx, gates, wgu, bgu, wd, bd, layer):
    n_rows = idx.shape[0]
    chunk = _moe_chunk(n_rows)
    tok, gw, goff, cnt, n_slots = _moe_metadata(idx, gates, chunk)
    assert n_slots % 1024 == 0 and MOE_TRASH >= MOE_TILE
    out_rows = (chunk + MOE_TRASH) * SUBLANES
    hbm = pl.BlockSpec(memory_space=pl.ANY)
    grid_spec = pltpu.PrefetchScalarGridSpec(
        num_scalar_prefetch=2,
        grid=(n_rows // chunk, N_EXPERTS),
        in_specs=[hbm, hbm,
                  pl.BlockSpec((chunk * SUBLANES, LANES), lambda c, e, *_: (c, 0), pipeline_mode=pl.Buffered(1)),
                  pl.BlockSpec((1, 1, D_MODEL, 2 * D_EXPERT), lambda c, e, *_: (layer, e, 0, 0)),
                  pl.BlockSpec((1, 1, 1, 2 * D_EXPERT), lambda c, e, *_: (layer, e, 0, 0)),
                  pl.BlockSpec((1, 1, D_EXPERT, D_MODEL), lambda c, e, *_: (layer, e, 0, 0)),
                  pl.BlockSpec((1, 1, 1, D_MODEL), lambda c, e, *_: (layer, e, 0, 0))],
        out_specs=pl.BlockSpec((out_rows, LANES), lambda c, e, *_: (c, 0), pipeline_mode=pl.Buffered(1)),
        scratch_shapes=[pltpu.VMEM((2 * TILE_ROWS, LANES), F32), pltpu.VMEM((2 * TILE_ROWS, LANES), F32),
                        pltpu.SMEM((n_slots + MOE_TILE,), jnp.int32), pltpu.SMEM((n_slots + MOE_TILE,), F32),
                        pltpu.SemaphoreType.DMA((2,))],
    )
    return pl.pallas_call(
        functools.partial(_moe_kernel, n_slots=n_slots, chunk=chunk),
        grid_spec=grid_spec,
        out_shape=jax.ShapeDtypeStruct((n_rows // chunk * out_rows, LANES), F32),
        compiler_params=_params(("arbitrary", "arbitrary")), name="moe",
    )(goff, cnt, tok, gw, f_moe, wgu, bgu, wd, bd), chunk


ATT_ROWS = 32
ATT_ROWS_UNROLL = 4


def _na_kernel(q_ref, k_ref, v_ref, kc_ref, vc_ref, bias_ref, o_ref, *, n_blocks, n_keys):
    blk = pl.program_id(2)
    key_row0 = jnp.clip(NA_QROWS * blk - NA_WIN_H // 2, 0, n_blocks * NA_QROWS - NA_KROWS)
    start = pl.multiple_of(key_row0 * GRID_W, GRID_W)
    q = q_ref[...]
    kl = k_ref[pl.ds(start, n_keys), :]
    vl = v_ref[pl.ds(start, n_keys), :]
    kc = kc_ref[...]
    vc = vc_ref[...]
    lane = lax.broadcasted_iota(jnp.int32, (1, LANES), 1)
    scale = NA_HEAD_DIM ** -0.5
    outs = []
    for a in range(2):
        sel = (lane < NA_HEAD_DIM) if a == 0 else (lane >= NA_HEAD_DIM)
        qa = jnp.where(sel, q, jnp.zeros_like(q)) * scale
        s_lat = _dot_nt(qa, kl) + bias_ref[0, a]
        s_ctx = _dot_nt(qa, kc)
        m = jnp.maximum(jnp.max(s_lat, axis=1, keepdims=True), jnp.max(s_ctx, axis=1, keepdims=True))
        e_lat = jnp.exp(s_lat - m)
        e_ctx = jnp.exp(s_ctx - m)
        den = jnp.sum(e_lat, axis=1, keepdims=True) + jnp.sum(e_ctx, axis=1, keepdims=True)
        outs.append((_dot(e_lat.astype(BF16), vl) + _dot(e_ctx.astype(BF16), vc)) / den)
    o_ref[...] = jnp.where(lane < NA_HEAD_DIM, outs[0], outs[1]).astype(BF16)


def _na_bias_table(rpb, rows):
    H = rpb.shape[0]
    qc = np.arange(GRID_W)
    col_start = np.clip(qc - NA_WIN_W // 2, 0, GRID_W - NA_WIN_W)
    col_in = (qc[None, :] >= col_start[:, None]) & (qc[None, :] < col_start[:, None] + NA_WIN_W)
    dx = np.clip(qc[None, :] - qc[:, None], 1 - NA_WIN_W, NA_WIN_W - 1) + NA_WIN_W - 1
    n_blocks = rows // NA_QROWS
    pick_y, masks = [], []
    for blk in (0, 1, n_blocks - 1):
        r = NA_QROWS * blk + np.arange(NA_QROWS)
        key_row0 = int(np.clip(NA_QROWS * blk - NA_WIN_H // 2, 0, rows - NA_KROWS))
        kr = key_row0 + np.arange(NA_KROWS)
        win0 = np.clip(r - NA_WIN_H // 2, 0, rows - NA_WIN_H)
        row_in = (kr[None, :] >= win0[:, None]) & (kr[None, :] < win0[:, None] + NA_WIN_H)
        dy = np.clip(kr[None, :] - r[:, None] + NA_WIN_H - 1, 0, 2 * NA_WIN_H - 2)
        pick_y.append(np.eye(2 * NA_WIN_H - 1, dtype=np.float32)[dy.reshape(-1)])
        masks.append(row_in[:, None, :, None] & col_in[None, :, None, :])
    pick_x = np.eye(2 * NA_WIN_W - 1, dtype=np.float32)[dx.reshape(-1)]
    bias = jnp.einsum('hyx,cay,bx->chab', rpb.astype(F32), jnp.asarray(np.stack(pick_y)), jnp.asarray(pick_x),
                      precision=lax.Precision.HIGHEST)
    bias = bias.reshape(3, H, NA_QROWS, NA_KROWS, GRID_W, GRID_W).transpose(0, 1, 2, 4, 3, 5)
    tabs = jnp.where(jnp.asarray(np.stack(masks))[:, None], bias, NEG_INF)
    tabs = tabs.reshape(3, H, NA_QROWS * GRID_W, NA_KROWS * GRID_W)
    return jnp.concatenate([tabs, jnp.full_like(tabs[:1], NEG_INF)], axis=0)


def _neighbourhood_attention(att, bias_tab, n_batch, n_lat, n_ctx):
    T = att.shape[0]
    qrows = NA_QROWS * GRID_W
    n_blocks = n_lat // qrows
    n_keys = NA_KROWS * GRID_W
    pairs = NA_WIDTH // LANES
    assert n_ctx == qrows
    lat_blocks = n_batch * n_blocks

    def q_map(b, hp, blk):
        return (jnp.where(blk < n_blocks, b * n_blocks + blk, lat_blocks + b), hp)

    def cls_map(b, hp, blk):
        cls = jnp.where(blk == 0, 0, jnp.where(blk == n_blocks - 1, 2, jnp.where(blk == n_blocks, 3, 1)))
        return (cls, hp, 0, 0)

    ctx_row = n_batch * n_lat // n_ctx
    return pl.pallas_call(
        functools.partial(_na_kernel, n_blocks=n_blocks, n_keys=n_keys),
        grid=(n_batch, pairs, n_blocks + 1),
        in_specs=[pl.BlockSpec((qrows, LANES), q_map),
                  pl.BlockSpec((n_lat, LANES), lambda b, hp, blk: (b, pairs + hp)),
                  pl.BlockSpec((n_lat, LANES), lambda b, hp, blk: (b, 2 * pairs + hp)),
                  pl.BlockSpec((n_ctx, LANES), lambda b, hp, blk: (ctx_row + b, pairs + hp)),
                  pl.BlockSpec((n_ctx, LANES), lambda b, hp, blk: (ctx_row + b, 2 * pairs + hp)),
                  pl.BlockSpec((1, 2, qrows, n_keys), cls_map)],
        out_specs=pl.BlockSpec((qrows, LANES), q_map),
        out_shape=jax.ShapeDtypeStruct((T, NA_WIDTH), BF16),
        compiler_params=_params(("arbitrary", "arbitrary", "arbitrary")), name="na_attention",
    )(att, att, att, att, att, bias_tab)


SWA_BLOCK = 128
SWA_BAND = 3 * SWA_BLOCK
def _swa_mask_table(n_lat):
    r = np.arange(SWA_BLOCK)[:, None]
    j = np.arange(SWA_BAND)[None, :]
    tabs = []
    n_blocks = n_lat // SWA_BLOCK
    for nb in (0, 1, n_blocks - 1):
        start = int(np.clip((nb - 1) * SWA_BLOCK, 0, n_lat - SWA_BAND))
        delta = nb * SWA_BLOCK + r - (start + j)
        tabs.append(np.where(np.abs(delta) <= SWA_WINDOW, 0.0, NEG_INF))
    tabs.append(np.full((SWA_BLOCK, SWA_BAND), NEG_INF))
    return jnp.asarray(np.stack(tabs), F32)


def _swa_kernel(sink_ref, q_ref, k_ref, v_ref, kc_ref, vc_ref, mask_ref, o_ref, s_ref, p_ref, den_ref, *, n_lat, group):
    kvh = pl.program_id(1)
    nb = pl.program_id(2)
    start = pl.multiple_of(jnp.clip((nb - 1) * SWA_BLOCK, 0, n_lat - SWA_BAND), SWA_BLOCK)
    kl = k_ref[pl.ds(start, SWA_BAND), :]
    vl = v_ref[pl.ds(start, SWA_BAND), :]
    q = q_ref[...]
    lane = lax.broadcasted_iota(jnp.int32, (1, LANES), 1)
    first_head = (lane & (SWA_HEAD_DIM - 1)) < SWA_HEAD_DIM // 2
    scale = SWA_HEAD_DIM ** -0.5
    parts = []
    for g in range(group):
        qp = q[:, (g // 2) * LANES:(g // 2 + 1) * LANES]
        sel = first_head if g % 2 == 0 else jnp.logical_not(first_head)
        parts.append(jnp.where(sel, qp, jnp.zeros_like(qp)) * scale)
    qs = jnp.concatenate(parts, axis=0)
    s_ref[:, :SWA_BAND] = _dot_nt(qs, kl)
    s_ref[:, SWA_BAND:] = _dot_nt(qs, kc_ref[...])

    def softmax_rows(i, carry):
        for u in range(ATT_ROWS_UNROLL):
            r0 = pl.multiple_of((i * ATT_ROWS_UNROLL + u) * ATT_ROWS, ATT_ROWS)
            rows = pl.ds(r0, ATT_ROWS)
            mrows = pl.ds(pl.multiple_of(r0 & (SWA_BLOCK - 1), ATT_ROWS), ATT_ROWS)
            s_lat = s_ref[rows, :SWA_BAND] + mask_ref[0, mrows, :]
            s_ctx = s_ref[rows, SWA_BAND:]
            sink = jnp.full((ATT_ROWS, 1), sink_ref[kvh * group + r0 // SWA_BLOCK], F32)
            m = jnp.maximum(jnp.maximum(jnp.max(s_lat, axis=1, keepdims=True), jnp.max(s_ctx, axis=1, keepdims=True)), sink)
            e_lat = jnp.exp(s_lat - m)
            e_ctx = jnp.exp(s_ctx - m)
            den = jnp.sum(e_lat, axis=1, keepdims=True) + jnp.sum(e_ctx, axis=1, keepdims=True) + jnp.exp(sink - m)
            p_ref[rows, :SWA_BAND] = e_lat.astype(BF16)
            p_ref[rows, SWA_BAND:] = e_ctx.astype(BF16)
            den_ref[rows, :] = jnp.broadcast_to(den, (ATT_ROWS, LANES))
        return carry

    lax.fori_loop(0, group * SWA_BLOCK // (ATT_ROWS * ATT_ROWS_UNROLL), softmax_rows, 0)
    o = (_dot(p_ref[:, :SWA_BAND], vl) + _dot(p_ref[:, SWA_BAND:], vc_ref[...])) / den_ref[...]
    for p in range(group // 2):
        o_a = o[(2 * p) * SWA_BLOCK:(2 * p + 1) * SWA_BLOCK]
        o_b = o[(2 * p + 1) * SWA_BLOCK:(2 * p + 2) * SWA_BLOCK]
        o_ref[:, p * LANES:(p + 1) * LANES] = jnp.where(lane < SWA_HEAD_DIM, o_a, o_b).astype(BF16)


def _window_attention(att, sink, n_batch, n_lat, n_ctx, with_ctx):
    T = att.shape[0]
    group = SWA_Q_HEADS // SWA_KV_HEADS
    qw = group * SWA_HEAD_DIM
    n_lat_blocks = n_lat // SWA_BLOCK
    n_ctx_blocks = n_ctx // SWA_BLOCK if with_ctx else 0
    k_col = SWA_Q_HEADS * SWA_HEAD_DIM // LANES
    v_col = k_col + SWA_KV_HEADS
    ctx_row = n_batch * n_lat // n_ctx

    def q_map(b, kvh, nb):
        lat = b * n_lat_blocks + nb
        ctx = n_batch * n_lat_blocks + b * (n_ctx // SWA_BLOCK) + (nb - n_lat_blocks)
        return (jnp.where(nb < n_lat_blocks, lat, ctx), kvh)

    def cls_map(b, kvh, nb):
        cls = jnp.where(nb == 0, 0, jnp.where(nb == n_lat_blocks - 1, 2, jnp.where(nb >= n_lat_blocks, 3, 1)))
        return (cls, 0, 0)

    grid_spec = pltpu.PrefetchScalarGridSpec(
        num_scalar_prefetch=0,
        grid=(n_batch, SWA_KV_HEADS, n_lat_blocks + n_ctx_blocks),
        in_specs=[pl.BlockSpec(memory_space=pltpu.SMEM),
                  pl.BlockSpec((SWA_BLOCK, qw), q_map),
                  pl.BlockSpec((n_lat, LANES), lambda b, kvh, nb: (b, k_col + kvh)),
                  pl.BlockSpec((n_lat, LANES), lambda b, kvh, nb: (b, v_col + kvh)),
                  pl.BlockSpec((n_ctx, LANES), lambda b, kvh, nb: (ctx_row + b, k_col + kvh)),
                  pl.BlockSpec((n_ctx, LANES), lambda b, kvh, nb: (ctx_row + b, v_col + kvh)),
                  pl.BlockSpec((1, SWA_BLOCK, SWA_BAND), cls_map)],
        out_specs=pl.BlockSpec((SWA_BLOCK, qw), q_map),
        scratch_shapes=[pltpu.VMEM((group * SWA_BLOCK, SWA_BAND + n_ctx), F32),
                        pltpu.VMEM((group * SWA_BLOCK, SWA_BAND + n_ctx), BF16),
                        pltpu.VMEM((group * SWA_BLOCK, LANES), F32)],
    )
    return pl.pallas_call(
        functools.partial(_swa_kernel, n_lat=n_lat, group=group),
        grid_spec=grid_spec,
        out_shape=jax.ShapeDtypeStruct((T if with_ctx else n_batch * n_lat, SWA_Q_HEADS * SWA_HEAD_DIM), BF16),
        compiler_params=_params(("arbitrary", "arbitrary", "arbitrary")), name="window_attention",
    )(sink.astype(F32), att, att, att, att, att, _swa_mask_table(n_lat))


DN_PAIR = 2 * DN_CHUNK
DN_GROUP = 4
DN_NQ = 2 * (DN_HEAD_DIM + DN_CHUNK)


def _bdot(a, b):
    return jnp.einsum('gik,gkj->gij', a, b, preferred_element_type=F32)


def _bdot_nt(a, b):
    return jnp.einsum('gik,gjk->gij', a, b, preferred_element_type=F32)


def _split2(x):
    hi = x.astype(BF16)
    return hi, (x - hi.astype(F32)).astype(BF16)


def _softplus(x):
    return jnp.maximum(x, 0.0) + jnp.log(1.0 + jnp.exp(-jnp.abs(x)))


def _pack_dot(a0, b0, a1, b1):
    z = jnp.zeros_like(b0)
    rhs = jnp.concatenate([jnp.concatenate([b0, z], axis=2), jnp.concatenate([z, b1], axis=2)], axis=1)
    r = _bdot(jnp.concatenate([a0, a1], axis=2), rhs)
    return r[:, :, :DN_PAIR], r[:, :, DN_PAIR:]


def _pack_dot3(a0, b0, a1, b1):
    (a0h, a0l), (b0h, b0l), (a1h, a1l), (b1h, b1l) = _split2(a0), _split2(b0), _split2(a1), _split2(b1)
    hh = _pack_dot(a0h, b0h, a1h, b1h)
    hl = _pack_dot(a0h, b0l, a1h, b1l)
    lh = _pack_dot(a0l, b0h, a1l, b1h)
    return hh[0] + (hl[0] + lh[0]), hh[1] + (hl[1] + lh[1])


def _dn_prepare(q, k, v, gcs, betas):
    G = q.shape[0]
    ii = lax.broadcasted_iota(jnp.int32, (DN_PAIR, DN_PAIR), 0)
    jj = lax.broadcasted_iota(jnp.int32, (DN_PAIR, DN_PAIR), 1)
    same = (ii >> 6) == (jj >> 6)
    incl = (same & (ii >= jj), same & (ii <= jj))
    strict = (same & (ii > jj), same & (ii < jj))
    eye = (ii == jj).astype(F32)
    eye16 = jnp.broadcast_to(eye.astype(BF16), (G, DN_PAIR, DN_PAIR))
    lane0 = jnp.broadcast_to((jj == 0).astype(BF16), (G, DN_PAIR, DN_PAIR))
    g_rows = sum(_bdot_nt(lane0, jnp.concatenate([pf, pb], axis=1)) for pf, pb in zip(_split3(gcs[0]), _split3(gcs[1])))
    g_row = (g_rows[:, :, :DN_PAIR], g_rows[:, :, DN_PAIR:])
    decay = [jnp.where(incl[d], jnp.exp(jnp.where(incl[d], gcs[d] - g_row[d], 0.0)), 0.0) for d in range(2)]
    k16 = k.astype(BF16)
    k_t = _bdot_nt(eye16, k16).astype(BF16)
    kb = [k * betas[d] for d in range(2)]
    vb = [v * betas[d] for d in range(2)]
    kk = _pack_dot(kb[0].astype(BF16), k_t, kb[1].astype(BF16), k_t)
    m = [jnp.where(strict[d], kk[d] * decay[d], 0.0) for d in range(2)]
    inv = [eye - m[0], eye - m[1]]
    p = m
    for _ in range(5):
        p = _pack_dot3(p[0], p[0], p[1], p[1])
        upd = _pack_dot3(inv[0], p[0], inv[1], p[1])
        inv = [inv[0] + upd[0], inv[1] + upd[1]]
    qk = _bdot(q.astype(BF16), k_t)
    first_chunk = lax.broadcasted_iota(jnp.int32, (DN_PAIR, 1), 0) < DN_CHUNK
    uws, a_intras, qgs, kgs, egls = [], [], [], [], []
    for d in range(2):
        gc = gcs[d]
        eg = jnp.exp(gc)
        uws.append(_bdot(inv[d].astype(BF16), jnp.concatenate([kb[d] * eg, vb[d]], axis=2).astype(BF16)).astype(BF16))
        a_intras.append(jnp.where(incl[d], qk * decay[d], 0.0).astype(BF16))
        lo, hi = (DN_CHUNK - 1, DN_PAIR - 1) if d == 0 else (0, DN_CHUNK)
        g_last = jnp.where(first_chunk, gc[:, lo:lo + 1, :], gc[:, hi:hi + 1, :])
        kgs.append((k * jnp.exp(g_last - gc)).astype(BF16))
        qgs.append(q * eg)
        egls.append(jnp.exp(g_last))
    kg_ts = _bdot_nt(eye16, jnp.concatenate(kgs, axis=1)).astype(BF16)
    zero = jnp.zeros((G, DN_PAIR, 2 * LANES), BF16)
    res = []
    for d in range(2):
        wu = uws[d]
        au = _bdot(a_intras[d], wu)
        q_eff = (qgs[d] - au[:, :, :LANES]).astype(BF16)
        wu_halves = jnp.concatenate([jnp.where(first_chunk, wu, zero), jnp.where(first_chunk, zero, wu)], axis=2)
        nb = _bdot(kg_ts[:, :, d * DN_PAIR:(d + 1) * DN_PAIR], wu_halves)
        nq = jnp.concatenate([(-nb[:, :, :LANES]).astype(BF16), q_eff[:, :DN_CHUNK],
                              (-nb[:, :, 2 * LANES:3 * LANES]).astype(BF16), q_eff[:, DN_CHUNK:]], axis=1)
        bb = jnp.concatenate([nb[:, :, LANES:2 * LANES], nb[:, :, 3 * LANES:]], axis=1)
        res.append((nq, bb, au[:, :, LANES:], egls[d]))
    return res


def _dn_sequence(q_ref, k_ref, v_ref, z_ref, ab_ref, cw_refs, gpar_ref, nw_ref, o_ref, s0, head, T, sc):
    (qn, kn, vn, gcs, bts, nqs, bss, egls, outs) = sc
    row = lax.broadcasted_iota(jnp.int32, (T, 1), 0)

    def conv_silu(x, w_ref):
        acc = x * w_ref[DN_CONV // 2:DN_CONV // 2 + 1, :]
        for j in range(DN_CONV):
            d = j - DN_CONV // 2
            if d == 0:
                continue
            shifted = pltpu.roll(x, (-d) % T, axis=0)
            ok = (row + d >= 0) & (row + d < T)
            acc = acc + jnp.where(ok, shifted, 0.0) * w_ref[j:j + 1, :]
        return acc * _sigmoid(acc)

    q = conv_silu(q_ref[...], cw_refs[0])
    k = conv_silu(k_ref[...], cw_refs[1])
    v = conv_silu(v_ref[...], cw_refs[2])
    q = q * lax.rsqrt(jnp.sum(q * q, axis=1, keepdims=True) + RMS_EPS) * DN_HEAD_DIM ** -0.5
    k = k * lax.rsqrt(jnp.sum(k * k, axis=1, keepdims=True) + RMS_EPS)
    qn[pl.ds(0, T), :] = q
    kn[pl.ds(0, T), :] = k
    vn[pl.ds(0, T), :] = v

    ab = ab_ref[...]
    g_all = -jnp.exp(gpar_ref[0:1, :]) * _softplus(ab + gpar_ref[1:2, :])
    b_all = _sigmoid(ab)
    lane = lax.broadcasted_iota(jnp.int32, (1, LANES), 1)
    src = jnp.where(lane < 2 * DN_HEADS, g_all, b_all)
    krow = lax.broadcasted_iota(jnp.int32, (3 * LANES, 4 * LANES), 0) & (LANES - 1)
    kcol = lax.broadcasted_iota(jnp.int32, (3 * LANES, 4 * LANES), 1) >> 7
    pick = (krow == kcol * DN_HEADS + head).astype(BF16)
    cols = _dot(jnp.concatenate(_split3(src), axis=1), pick)

    pos = row & (DN_CHUNK - 1)
    for d in range(2):
        g = cols[:, d * LANES:(d + 1) * LANES]
        for sh in (1, 2, 4, 8, 16, 32):
            if d == 0:
                g = g + jnp.where(pos >= sh, pltpu.roll(g, sh, axis=0), 0.0)
            else:
                g = g + jnp.where(pos < DN_CHUNK - sh, pltpu.roll(g, T - sh, axis=0), 0.0)
        gcs[d][pl.ds(0, T), :] = g
        bts[d][pl.ds(0, T), :] = cols[:, (2 + d) * LANES:(3 + d) * LANES]

    n_pairs = T // DN_PAIR
    G = min(DN_GROUP, n_pairs)
    R = G * DN_PAIR

    def prepare(gi, carry):
        rows = pl.ds(pl.multiple_of(gi * R, R), R)
        shape = (G, DN_PAIR, LANES)
        qq = qn[rows, :].reshape(shape)
        kk = kn[rows, :].reshape(shape)
        vv = vn[rows, :].reshape(shape)
        res = _dn_prepare(qq, kk, vv, [gcs[d][rows, :].reshape(shape) for d in range(2)],
                          [bts[d][rows, :].reshape(shape) for d in range(2)])
        for d in range(2):
            nq, bb, o_const, egl = res[d]
            nqs[d][pl.ds(pl.multiple_of(gi * G * DN_NQ, G * DN_NQ), G * DN_NQ), :] = nq.reshape(G * DN_NQ, LANES)
            bss[d][pl.ds(pl.multiple_of(gi * 2 * R, 2 * R), 2 * R), :] = bb.reshape(2 * R, LANES)
            outs[d][rows, :] = o_const.reshape(R, LANES)
            egls[d][rows, :] = egl.reshape(R, LANES)
        return carry

    lax.fori_loop(0, n_pairs // G, prepare, 0)

    def chunk_step(S, pair, half, d):
        off = pl.multiple_of(pair * DN_PAIR + half * DN_CHUNK, DN_CHUNK)
        rows = pl.ds(off, DN_CHUNK)
        lhs = nqs[d][pl.ds(pl.multiple_of(pair * DN_NQ + half * (DN_NQ // 2), DN_NQ // 2), DN_NQ // 2), :]
        r = _dot(lhs, S.astype(BF16))
        outs[d][rows, :] = outs[d][rows, :] + r[DN_HEAD_DIM:]
        decay = jnp.broadcast_to(egls[d][pl.ds(off, 1), :], (DN_HEAD_DIM, LANES))
        add = bss[d][pl.ds(pl.multiple_of(pair * 2 * DN_HEAD_DIM + half * DN_HEAD_DIM, DN_HEAD_DIM), DN_HEAD_DIM), :]
        return S * decay + (r[:DN_HEAD_DIM] + add)

    def scan(p, carry):
        s_f, s_b = carry
        pb = n_pairs - 1 - p
        s_f = chunk_step(s_f, p, 0, 0)
        s_b = chunk_step(s_b, pb, 1, 1)
        s_f = chunk_step(s_f, p, 1, 0)
        s_b = chunk_step(s_b, pb, 0, 1)
        return s_f, s_b

    s_f, s_b = lax.fori_loop(0, n_pairs, scan, s0)

    o = outs[0][pl.ds(0, T), :] + outs[1][pl.ds(0, T), :]
    o = o * lax.rsqrt(jnp.mean(o * o, axis=1, keepdims=True) + RMS_EPS) * nw_ref[...]
    z = z_ref[...]
    o_ref[...] = (o * (z * _sigmoid(z))).astype(BF16)
    return s_f, s_b


def _dn_kernel(ql, kl, vl, zl, abl, qc, kc, vc, zc, abc, cwq, cwk, cwv, gpar, nw, ol, oc, *scratch, n_lat, n_ctx):
    head = pl.program_id(1)
    it = iter(scratch)
    sc = [next(it) if i < 3 else (next(it), next(it)) for i in range(9)]
    zero = jnp.zeros((DN_HEAD_DIM, DN_HEAD_DIM), F32)
    s_ctx = _dn_sequence(qc, kc, vc, zc, abc, (cwq, cwk, cwv), gpar, nw, oc, (zero, zero), head, n_ctx, sc)
    _dn_sequence(ql, kl, vl, zl, abl, (cwq, cwk, cwv), gpar, nw, ol, s_ctx, head, n_lat, sc)


def _delta_net(dn, conv_w, a_log, dt_bias, norm_w, n_batch, n_lat, n_ctx):
    H = DN_HEADS
    ctx_row = n_batch * n_lat // n_ctx
    ab_col = 4 * H

    def lat(off):
        return pl.BlockSpec((n_lat, LANES), lambda b, h: (b, off + h))

    def ctx(off):
        return pl.BlockSpec((n_ctx, LANES), lambda b, h: (ctx_row + b, off + h))

    cw = jnp.zeros((SUBLANES, 3 * DN_WIDTH), F32).at[:DN_CONV].set(conv_w.astype(F32))
    gpar = jnp.zeros((SUBLANES, LANES), F32)
    gpar = gpar.at[0, :2 * H].set(a_log.reshape(-1).astype(F32)).at[1, :2 * H].set(dt_bias.reshape(-1).astype(F32))
    in_specs = [lat(0), lat(H), lat(2 * H), lat(3 * H), pl.BlockSpec((n_lat, LANES), lambda b, h: (b, ab_col)),
                ctx(0), ctx(H), ctx(2 * H), ctx(3 * H), pl.BlockSpec((n_ctx, LANES), lambda b, h: (ctx_row + b, ab_col)),
                pl.BlockSpec((SUBLANES, LANES), lambda b, h: (0, h)),
                pl.BlockSpec((SUBLANES, LANES), lambda b, h: (0, H + h)),
                pl.BlockSpec((SUBLANES, LANES), lambda b, h: (0, 2 * H + h)),
                pl.BlockSpec((SUBLANES, LANES), lambda b, h: (0, 0)),
                pl.BlockSpec((1, LANES), lambda b, h: (0, 0))]
    f32s = lambda rows=n_lat: pltpu.VMEM((rows, LANES), F32)
    n_pairs = n_lat // DN_PAIR
    scratch = [f32s(), f32s(), f32s()]
    scratch += [f32s() for _ in range(4)]
    scratch += [pltpu.VMEM((n_pairs * DN_NQ, LANES), BF16) for _ in range(2)]
    scratch += [f32s(2 * n_pairs * DN_HEAD_DIM) for _ in range(2)]
    scratch += [f32s() for _ in range(4)]
    return pl.pallas_call(
        functools.partial(_dn_kernel, n_lat=n_lat, n_ctx=n_ctx),
        grid=(n_batch, H),
        in_specs=in_specs,
        out_specs=[pl.BlockSpec((n_lat, LANES), lambda b, h: (b, h)),
                   pl.BlockSpec((n_ctx, LANES), lambda b, h: (b, h))],
        out_shape=[jax.ShapeDtypeStruct((n_batch * n_lat, DN_WIDTH), BF16),
                   jax.ShapeDtypeStruct((n_batch * n_ctx, DN_WIDTH), BF16)],
        scratch_shapes=scratch,
        compiler_params=_params(("arbitrary", "arbitrary")), name="delta_net",
    )(dn, dn, dn, dn, dn, dn, dn, dn, dn, dn, cw, cw, cw, gpar, norm_w.reshape(1, -1).astype(F32))


def _rope_tables(n_tokens):
    t = jnp.arange(n_tokens)
    n_freq = SWA_HEAD_DIM // 4
    inv = ROPE_THETA ** (-jnp.arange(n_freq, dtype=F32) / n_freq)
    row = (t // GRID_W).astype(F32)[:, None]
    col = (t % GRID_W).astype(F32)[:, None]
    ang = jnp.concatenate([row * inv, col * inv], -1)
    cos, sin = jnp.cos(ang), jnp.sin(ang)
    cos_t = jnp.concatenate([cos, cos, cos, cos], -1)
    sin_t = jnp.concatenate([-sin, -sin, sin, sin], -1)
    ident = jnp.ones((TOK_TILE, LANES), F32)
    return (jnp.concatenate([cos_t, ident], 0), jnp.concatenate([sin_t, jnp.zeros_like(ident)], 0))


def _swa_weight(w_in):
    D = w_in.shape[0]
    qw = SWA_Q_HEADS * SWA_HEAD_DIM
    kw = SWA_KV_HEADS * SWA_HEAD_DIM
    half = SWA_HEAD_DIM // 2
    wq = w_in[:, :qw].reshape(D, SWA_Q_HEADS // 2, 2, 2, half)
    wq = jnp.transpose(wq, (0, 1, 3, 2, 4)).reshape(D, qw)
    wk = w_in[:, qw:qw + kw].reshape(D, SWA_KV_HEADS, 2, 1, half)
    wk = jnp.broadcast_to(wk, (D, SWA_KV_HEADS, 2, 2, half)).reshape(D, 2 * kw)
    wv = w_in[:, qw + kw:].reshape(D, SWA_KV_HEADS, 1, SWA_HEAD_DIM)
    wv = jnp.broadcast_to(wv, (D, SWA_KV_HEADS, 2, SWA_HEAD_DIM)).reshape(D, 2 * kw)
    return jnp.concatenate([wq, wk, wv], axis=1).astype(BF16)


def kernel(x, c, ctx, c_ctx, w_mod, b_mod, ln_g, ln_b, w_in_ab, na_rpb, dn_conv, dn_a_log, dn_dt_bias, dn_norm_w,
           w_out_ab, w_in_c, swa_sink, w_out_c, w_router, b_router, w_gu, b_gu, w_down, b_down):
    B, N, D = x.shape
    L = ctx.shape[1]
    depth = w_mod.shape[0]
    alpha = (2 * depth) ** 0.25
    assert D == D_MODEL and N % TOK_TILE == 0 and L == TOK_TILE
    geom = (B * N // TOK_TILE, N // TOK_TILE, B)

    mod_rows = -(-(B + 1) // 16) * 16
    cc = jnp.zeros((mod_rows, D), F32).at[:B].set(c).at[B].set(c_ctx)
    mod_all = _modulation(cc, w_mod, b_mod)
    cos_t, sin_t = _rope_tables(N)
    n_att_ab = 3 * NA_WIDTH
    ab_main = n_att_ab + 4 * DN_WIDTH
    wr = jnp.zeros((depth, D, LANES), BF16).at[:, :, :N_EXPERTS].set(w_router.astype(BF16))
    br = jnp.full((depth, 1, LANES), NEG_INF, F32).at[:, 0, :N_EXPERTS].set(b_router.astype(F32))
    wgu16, wd16 = w_gu.astype(BF16), w_down.astype(BF16)
    bgu4, bd4 = b_gu[:, :, None, :].astype(F32), b_down[:, :, None, :].astype(F32)

    h = jnp.concatenate([x.reshape(B * N, D), ctx.reshape(B * L, D)], axis=0)
    moe_prev = mod_prev = ln_prev = None
    out = None
    for layer in range(depth):
        last = layer == depth - 1
        i = layer // 2
        mod = mod_all[layer].reshape(mod_rows, 1, -1)
        if layer % 2 == 0:
            w = w_in_ab[i]
            w = jnp.concatenate([w[:, :ab_main], jnp.pad(w[:, ab_main:], ((0, 0), (0, LANES - (w.shape[1] - ab_main))))], 1)
            h, att, dn = _proj_in(h, moe_prev, mod_prev, ln_prev, mod, w.astype(BF16), geom, n_att=n_att_ab, alpha=alpha)
            o_a = _neighbourhood_attention(att, _na_bias_table(na_rpb[i], N // GRID_W), B, N, L)
            o_bl, o_bc = _delta_net(dn, dn_conv[i], dn_a_log[i], dn_dt_bias[i], dn_norm_w[i], B, N, L)
            xs = [o_a, jnp.concatenate([o_bl, o_bc], axis=0)]
            wo = w_out_ab[i].astype(BF16)
            ws = [wo[:NA_WIDTH], wo[NA_WIDTH:]]
        else:
            w = _swa_weight(w_in_c[i])
            n_rope = (SWA_Q_HEADS + 2 * SWA_KV_HEADS) * SWA_HEAD_DIM // LANES
            h, att, _ = _proj_in(h, moe_prev, mod_prev, ln_prev, mod, w, geom, n_att=w.shape[1],
                                 rope=(cos_t, sin_t, n_rope), alpha=alpha)
            xs = [_window_attention(att, swa_sink[i], B, N, L, with_ctx=not last)]
            ws = [w_out_c[i].astype(BF16)]
        n_rows = B * N if last else B * (N + L)
        ln1 = jnp.stack([ln_g[layer, 0], ln_b[layer, 0]])
        ln2 = jnp.stack([ln_g[layer, 1], ln_b[layer, 1]])
        h1, f_moe, idx, gates = _proj_out(xs, ws, h, mod, ln1, wr[layer], br[layer], geom, n_rows, alpha=alpha)
        y_moe = _moe(f_moe, idx, gates, wgu16, bgu4, wd16, bd4, layer)
        if last:
            out = _final_combine(h1, y_moe, mod, ln2, geom, alpha=alpha).reshape(B, N, D)
        else:
            h, moe_prev, mod_prev, ln_prev = h1, y_moe, mod, ln2
    return out
```

```python
import functools

import jax
import jax.numpy as jnp
import numpy as np
from jax import lax
from jax.experimental import pallas as pl
from jax.experimental.pallas import tpu as pltpu

F32 = jnp.float32
BF16 = jnp.bfloat16

D_MODEL = 1024
GRID_W = 64
NA_HEADS = 8
NA_HEAD_DIM = 64
NA_WIN_H = 8
NA_WIN_W = 16
DN_HEADS = 4
DN_HEAD_DIM = 128
DN_CONV = 5
DN_CHUNK = 64
SWA_Q_HEADS = 16
SWA_KV_HEADS = 2
SWA_HEAD_DIM = 64
SWA_WINDOW = 128
ROPE_THETA = 10000.0
N_EXPERTS = 32
TOP_K = 4
D_EXPERT = 1024
SWIGLU_LIMIT = 7.0
SWIGLU_ALPHA = 1.702
LN_EPS = 1e-5
RMS_EPS = 1e-6
NEG_INF = -1e30

LANES = 128
SUBLANES = 8
TOK_TILE = 256
MOE_TILE = 128
MOE_CHUNK_MAX = 4608
VMEM_LIMIT = 60 * 1024 * 1024

NA_WIDTH = NA_HEADS * NA_HEAD_DIM
DN_WIDTH = DN_HEADS * DN_HEAD_DIM
NA_QROWS = 4
NA_KROWS = 12


def _params(sem):
    return pltpu.CompilerParams(dimension_semantics=sem, vmem_limit_bytes=VMEM_LIMIT)


def _sigmoid(x):
    return 1.0 / (1.0 + jnp.exp(-x))


def _dot(a, b):
    return jnp.dot(a, b, preferred_element_type=F32)


def _dot_nt(a, b):
    return lax.dot_general(a, b, (((1,), (1,)), ((), ())), preferred_element_type=F32)


def _moe_layout_load(ref, rows):
    return jnp.concatenate([ref[pl.ds(s, rows, stride=SUBLANES), :] for s in range(SUBLANES)], axis=1)


def _moe_layout_store(ref, val, rows):
    for s in range(SUBLANES):
        ref[pl.ds(s, rows, stride=SUBLANES), :] = val[:, s * LANES:(s + 1) * LANES]


def _layer_norm(y, g, b):
    mu = jnp.mean(y, axis=-1, keepdims=True)
    d = y - mu
    var = jnp.mean(d * d, axis=-1, keepdims=True)
    return d * lax.rsqrt(var + LN_EPS) * g + b


def _mod_kernel(s_ref, w_ref, b_ref, o_ref):
    s = s_ref[...]
    s = s * _sigmoid(s)
    o_ref[0] = _dot(s.astype(BF16), w_ref[0].astype(BF16)) + b_ref[0]


def _modulation(cc, w_mod, b_mod):
    depth = w_mod.shape[0]
    rows = cc.shape[0]
    nblk = w_mod.shape[2] // D_MODEL
    return pl.pallas_call(
        _mod_kernel,
        grid=(depth, nblk),
        in_specs=[pl.BlockSpec((rows, D_MODEL), lambda l, j: (0, 0)),
                  pl.BlockSpec((1, D_MODEL, D_MODEL), lambda l, j: (l, 0, j)),
                  pl.BlockSpec((1, 1, D_MODEL), lambda l, j: (l, 0, j))],
        out_specs=pl.BlockSpec((1, rows, D_MODEL), lambda l, j: (l, 0, j)),
        out_shape=jax.ShapeDtypeStruct((depth, rows, nblk * D_MODEL), F32),
        compiler_params=_params(("arbitrary", "arbitrary")),
        name="modulation",
    )(cc, w_mod, b_mod.reshape(depth, 1, -1))


def _pin_kernel(*refs, first, n_rope, n_att, alpha):
    refs = list(refs)
    h_ref = refs.pop(0)
    if not first:
        moe_ref, g2_ref, lg_ref, lb_ref = refs[:4]
        refs = refs[4:]
    sc_ref, sh_ref, w_ref = refs[:3]
    refs = refs[3:]
    if n_rope:
        cos_ref, sin_ref = refs[:2]
        refs = refs[2:]
    if not first:
        hout_ref = refs.pop(0)
    att_ref = refs.pop(0)
    dn_ref = refs.pop(0) if refs else None

    h = h_ref[...]
    if not first:
        m = _moe_layout_load(moe_ref, TOK_TILE)
        h = _layer_norm(alpha * h + g2_ref[0] * m, lg_ref[...], lb_ref[...])
        hout_ref[...] = h
    a = h * (1.0 + sc_ref[0]) + sh_ref[0]
    acc = _dot(a.astype(BF16), w_ref[...])
    if n_rope:
        cos = cos_ref[...]
        sin = sin_ref[...]
        for g in range(n_att // LANES):
            blk = acc[:, g * LANES:(g + 1) * LANES]
            if g < n_rope:
                blk = blk * cos + pltpu.roll(blk, LANES // 2, axis=1) * sin
            att_ref[:, g * LANES:(g + 1) * LANES] = blk.astype(BF16)
    else:
        att_ref[...] = acc[:, :n_att].astype(BF16)
    if dn_ref is not None:
        dn_ref[...] = acc[:, n_att:]


def _mod_spec(k, n_lat_tiles, lat_tiles_per_batch, n_batch):
    def imap(i):
        return (jnp.where(i < n_lat_tiles, i // lat_tiles_per_batch, n_batch), 0, k)
    return pl.BlockSpec((1, 1, D_MODEL), imap)


def _proj_in(h, moe_prev, mod_prev, ln_prev, mod, w, geom, *, n_att, rope=None, alpha=1.0):
    T = h.shape[0]
    n_tiles = T // TOK_TILE
    n_lat_tiles, per_batch, n_batch = geom
    first = moe_prev is None
    ncols = w.shape[1]
    row = pl.BlockSpec((TOK_TILE, D_MODEL), lambda i: (i, 0))
    vec = pl.BlockSpec((1, D_MODEL), lambda i: (0, 0))
    ms = functools.partial(_mod_spec, n_lat_tiles=n_lat_tiles, lat_tiles_per_batch=per_batch, n_batch=n_batch)
    args, specs = [h], [row]
    if not first:
        moe_arr, moe_chunk = moe_prev
        args += [moe_arr, mod_prev, ln_prev[0:1], ln_prev[1:2]]
        specs += [pl.BlockSpec((TOK_TILE * SUBLANES, LANES), lambda i: (_moe_block(i, moe_chunk), 0)), ms(5), vec, vec]
    args += [mod, mod, w]
    specs += [ms(1), ms(0), pl.BlockSpec((D_MODEL, ncols), lambda i: (0, 0))]
    n_rope = 0
    if rope is not None:
        cos_t, sin_t, n_rope = rope
        tab = pl.BlockSpec((TOK_TILE, LANES), lambda i: (jnp.where(i < n_lat_tiles, i % per_batch, per_batch), 0))
        args += [cos_t, sin_t]
        specs += [tab, tab]
    out_shapes, out_specs = [], []
    if not first:
        out_shapes.append(jax.ShapeDtypeStruct((T, D_MODEL), F32))
        out_specs.append(row)
    out_shapes.append(jax.ShapeDtypeStruct((T, n_att), BF16))
    out_specs.append(pl.BlockSpec((TOK_TILE, n_att), lambda i: (i, 0)))
    if ncols > n_att:
        out_shapes.append(jax.ShapeDtypeStruct((T, ncols - n_att), F32))
        out_specs.append(pl.BlockSpec((TOK_TILE, ncols - n_att), lambda i: (i, 0)))
    outs = pl.pallas_call(
        functools.partial(_pin_kernel, first=first, n_rope=n_rope, n_att=n_att, alpha=alpha),
        grid=(n_tiles,), in_specs=specs, out_specs=out_specs, out_shape=out_shapes,
        compiler_params=_params(("arbitrary",)), name="proj_in",
    )(*args)
    outs = list(outs)
    h_new = h if first else outs.pop(0)
    att = outs.pop(0)
    dn = outs.pop(0) if outs else None
    return h_new, att, dn


def _final_kernel(h_ref, moe_ref, g2_ref, lg_ref, lb_ref, o_ref, *, alpha):
    m = _moe_layout_load(moe_ref, TOK_TILE)
    o_ref[...] = _layer_norm(alpha * h_ref[...] + g2_ref[0] * m, lg_ref[...], lb_ref[...])


def _final_combine(h, moe, mod, ln, geom, *, alpha):
    T = h.shape[0]
    n_lat_tiles, per_batch, n_batch = geom
    moe, moe_chunk = moe
    row = pl.BlockSpec((TOK_TILE, D_MODEL), lambda i: (i, 0))
    vec = pl.BlockSpec((1, D_MODEL), lambda i: (0, 0))
    return pl.pallas_call(
        functools.partial(_final_kernel, alpha=alpha),
        grid=(T // TOK_TILE,),
        in_specs=[row, pl.BlockSpec((TOK_TILE * SUBLANES, LANES), lambda i: (_moe_block(i, moe_chunk), 0)),
                  _mod_spec(5, n_lat_tiles, per_batch, n_batch), vec, vec],
        out_specs=row, out_shape=jax.ShapeDtypeStruct((T, D_MODEL), F32),
        compiler_params=_params(("arbitrary",)), name="final_combine",
    )(h, moe, mod, ln[0:1], ln[1:2])


def _pout_kernel(*refs, n_in, alpha):
    xs = refs[:n_in]
    ws = refs[n_in:2 * n_in]
    (h_ref, g1_ref, lg_ref, lb_ref, sc_ref, sh_ref, wr_ref, br_ref,
     hout_ref, f_ref, idx_ref, gate_ref) = refs[2 * n_in:]
    m = _dot(xs[0][...], ws[0][...])
    for x_ref, w_ref in zip(xs[1:], ws[1:]):
        m = m + _dot(x_ref[...], w_ref[...])
    h1 = _layer_norm(alpha * h_ref[...] + g1_ref[0] * m, lg_ref[...], lb_ref[...])
    hout_ref[...] = h1
    f = h1 * (1.0 + sc_ref[0]) + sh_ref[0]
    _moe_layout_store(f_ref, f, TOK_TILE)
    logits = _dot(f.astype(BF16), wr_ref[...]) + br_ref[...]
    lane = lax.broadcasted_iota(jnp.int32, logits.shape, 1)
    work = logits
    vals, idxs = [], []
    for _ in range(TOP_K):
        v = jnp.max(work, axis=1, keepdims=True)
        ix = jnp.min(jnp.where(work == v, lane, LANES), axis=1, keepdims=True)
        vals.append(v)
        idxs.append(ix)
        work = jnp.where(lane == ix, -jnp.inf, work)
    es = [jnp.exp(v - vals[0]) for v in vals]
    den = es[0]
    for e in es[1:]:
        den = den + e
    idx_out = jnp.zeros(logits.shape, jnp.int32)
    gate_out = jnp.zeros(logits.shape, F32)
    for k in range(TOP_K):
        idx_out = jnp.where(lane == k, idxs[k], idx_out)
        gate_out = jnp.where(lane == k, es[k] / den, gate_out)
    idx_ref[...] = idx_out
    gate_ref[...] = gate_out


def _proj_out(xs, ws, h, mod, ln, w_router, b_router, geom, n_rows, *, alpha):
    n_lat_tiles, per_batch, n_batch = geom
    row = pl.BlockSpec((TOK_TILE, D_MODEL), lambda i: (i, 0))
    vec = pl.BlockSpec((1, D_MODEL), lambda i: (0, 0))
    lane_row = pl.BlockSpec((TOK_TILE, LANES), lambda i: (i, 0))
    ms = functools.partial(_mod_spec, n_lat_tiles=n_lat_tiles, lat_tiles_per_batch=per_batch, n_batch=n_batch)
    specs = [pl.BlockSpec((TOK_TILE, x.shape[1]), lambda i: (i, 0)) for x in xs]
    specs += [pl.BlockSpec(w.shape, lambda i: (0, 0)) for w in ws]
    specs += [row, ms(2), vec, vec, ms(4), ms(3),
              pl.BlockSpec((D_MODEL, LANES), lambda i: (0, 0)), pl.BlockSpec((1, LANES), lambda i: (0, 0))]
    return pl.pallas_call(
        functools.partial(_pout_kernel, n_in=len(xs), alpha=alpha),
        grid=(n_rows // TOK_TILE,), in_specs=specs,
        out_specs=[row, pl.BlockSpec((TOK_TILE * SUBLANES, LANES), lambda i: (i, 0)), lane_row, lane_row],
        out_shape=[jax.ShapeDtypeStruct((n_rows, D_MODEL), F32),
                   jax.ShapeDtypeStruct((n_rows * SUBLANES, LANES), F32),
                   jax.ShapeDtypeStruct((n_rows, LANES), jnp.int32),
                   jax.ShapeDtypeStruct((n_rows, LANES), F32)],
        compiler_params=_params(("arbitrary",)), name="proj_out",
    )(*xs, *ws, h, mod, ln[0:1], ln[1:2], mod, mod, w_router, b_router)


def _split3(x):
    hi = x.astype(BF16)
    r = x - hi.astype(F32)
    mid = r.astype(BF16)
    lo = (r - mid.astype(F32)).astype(BF16)
    return hi, mid, lo


MOE_UNROLL = 8


MOE_TRASH = TOK_TILE
TILE_ROWS = MOE_TILE * SUBLANES


def _moe_kernel(goff_ref, cnt_ref, tok_hbm, gw_hbm, f_ref, wgu_ref, bgu_ref, wd_ref, bd_ref,
                o_ref, xin_ref, y_ref, tok_s, gw_s, sem, *, n_slots, chunk):
    c = pl.program_id(0)
    e = pl.program_id(1)

    @pl.when(e == 0)
    def _():
        src = pl.ds(pl.multiple_of(c * n_slots, 1024), n_slots)
        cp_tok = pltpu.make_async_copy(tok_hbm.at[src], tok_s.at[pl.ds(0, n_slots)], sem.at[0])
        cp_gw = pltpu.make_async_copy(gw_hbm.at[src], gw_s.at[pl.ds(0, n_slots)], sem.at[1])
        cp_tok.start()
        cp_gw.start()
        o_ref[...] = jnp.zeros(o_ref.shape, F32)
        cp_tok.wait()
        cp_gw.wait()

    @pl.when((c == 0) & (e == 0))
    def _():
        xin_ref[...] = jnp.zeros(xin_ref.shape, F32)
        y_ref[...] = jnp.zeros(y_ref.shape, F32)
        for j in range(MOE_TILE):
            tok_s[n_slots + j] = 0
            gw_s[n_slots + j] = 0.0

    g = c * N_EXPERTS + e
    group0 = goff_ref[g]
    n_rows = cnt_ref[g]
    n_tiles = (n_rows + MOE_TILE - 1) // MOE_TILE

    def rows8(i):
        return pl.ds(pl.multiple_of(i * SUBLANES, SUBLANES), SUBLANES)

    def gather(parity, off):
        for j in range(MOE_TILE):
            t = jnp.minimum(tok_s[off + j], chunk - 1)
            xin_ref[rows8(parity * MOE_TILE + j), :] = f_ref[rows8(t), :]

    def scatter(parity, off, live):
        for j0 in range(0, MOE_TILE, MOE_UNROLL):
            ts, gs = [], []
            for j in range(j0, j0 + MOE_UNROLL):
                t, gwt = tok_s[off + j], gw_s[off + j]
                if live is not None:
                    t = jnp.where(live, t, chunk + j)
                    gwt = jnp.where(live, gwt, 0.0)
                ts.append(rows8(t))
                gs.append(gwt)
            olds = [o_ref[d, :] for d in ts]
            for u, (d, gwt, old) in enumerate(zip(ts, gs, olds)):
                o_ref[d, :] = old + gwt * y_ref[rows8(parity * MOE_TILE + j0 + u), :]

    @pl.when(n_rows > 0)
    def _():
        gather(0, group0)

        def tile(r, carry):
            parity = r & 1
            off = group0 + r * MOE_TILE
            gather(1 - parity, off + MOE_TILE)
            base = parity * TILE_ROWS
            x = jnp.concatenate([xin_ref[pl.ds(base + s, MOE_TILE, stride=SUBLANES), :] for s in range(SUBLANES)],
                                axis=1).astype(BF16)
            gu = _dot(x, wgu_ref[0, 0]) + bgu_ref[0, 0]
            gate = jnp.minimum(gu[:, :D_EXPERT], SWIGLU_LIMIT)
            up = jnp.clip(gu[:, D_EXPERT:], -SWIGLU_LIMIT, SWIGLU_LIMIT)
            hid = gate * _sigmoid(SWIGLU_ALPHA * gate) * (up + 1.0)
            y = _dot(hid.astype(BF16), wd_ref[0, 0]) + bd_ref[0, 0]
            scatter(1 - parity, jnp.maximum(off - MOE_TILE, 0), r > 0)
            for s in range(SUBLANES):
                y_ref[pl.ds(base + s, MOE_TILE, stride=SUBLANES), :] = y[:, s * LANES:(s + 1) * LANES]
            return carry

        lax.fori_loop(0, n_tiles, tile, 0)
        last = n_tiles - 1
        scatter(last & 1, group0 + last * MOE_TILE, None)


def _moe_chunk(n_rows):
    c = MOE_CHUNK_MAX
    while n_rows % c:
        c -= TOK_TILE
    return c


def _moe_block(i, chunk):
    per = chunk // TOK_TILE
    return (i // per) * (per + MOE_TRASH // TOK_TILE) + i % per


def _moe_metadata(idx, gates, chunk):
    n_rows = idx.shape[0]
    n_chunks = n_rows // chunk
    n_assign = chunk * TOP_K
    n_pad = N_EXPERTS * MOE_TILE
    e = idx[:, :TOP_K].reshape(n_chunks, n_assign)
    gv = gates[:, :TOP_K].reshape(n_chunks, n_assign)
    tokv = jnp.broadcast_to(jnp.arange(n_assign, dtype=jnp.int32) // TOP_K, (n_chunks, n_assign))
    pad_e = jnp.arange(n_pad, dtype=jnp.int32) // MOE_TILE
    pad_t = chunk + jnp.arange(n_pad, dtype=jnp.int32) % MOE_TILE
    keys = jnp.concatenate([2 * e, jnp.broadcast_to(2 * pad_e + 1, (n_chunks, n_pad))], axis=1)
    toks = jnp.concatenate([tokv, jnp.broadcast_to(pad_t, (n_chunks, n_pad))], axis=1)
    gws = jnp.concatenate([gv, jnp.zeros((n_chunks, n_pad), F32)], axis=1)
    _, tok, gw = lax.sort((keys, toks, gws), dimension=1, num_keys=1, is_stable=True)
    cnt = jnp.sum((e[:, :, None] == jnp.arange(N_EXPERTS, dtype=jnp.int32)).astype(jnp.int32), axis=1)
    goff = jnp.cumsum(cnt + MOE_TILE, axis=1) - (cnt + MOE_TILE)
    return tok.reshape(-1), gw.reshape(-1), goff.reshape(-1), cnt.reshape(-1), n_assign + n_pad


def _moe(f_moe, idx, gates, wgu, bgu, wd, bd, layer):
    n_rows = idx.shape[0]
    chunk = _moe_chunk(n_rows)
    tok, gw, goff, cnt, n_slots = _moe_metadata(idx, gates, chunk)
    assert n_slots % 1024 == 0 and MOE_TRASH >= MOE_TILE
    out_rows = (chunk + MOE_TRASH) * SUBLANES
    hbm = pl.BlockSpec(memory_space=pl.ANY)
    grid_spec = pltpu.PrefetchScalarGridSpec(
        num_scalar_prefetch=2,
        grid=(n_rows // chunk, N_EXPERTS),
        in_specs=[hbm, hbm,
                  pl.BlockSpec((chunk * SUBLANES, LANES), lambda c, e, *_: (c, 0), pipeline_mode=pl.Buffered(1)),
                  pl.BlockSpec((1, 1, D_MODEL, 2 * D_EXPERT), lambda c, e, *_: (layer, e, 0, 0)),
                  pl.BlockSpec((1, 1, 1, 2 * D_EXPERT), lambda c, e, *_: (layer, e, 0, 0)),
                  pl.BlockSpec((1, 1, D_EXPERT, D_MODEL), lambda c, e, *_: (layer, e, 0, 0)),
                  pl.BlockSpec((1, 1, 1, D_MODEL), lambda c, e, *_: (layer, e, 0, 0))],
        out_specs=pl.BlockSpec((out_rows, LANES), lambda c, e, *_: (c, 0), pipeline_mode=pl.Buffered(1)),
        scratch_shapes=[pltpu.VMEM((2 * TILE_ROWS, LANES), F32), pltpu.VMEM((2 * TILE_ROWS, LANES), F32),
                        pltpu.SMEM((n_slots + MOE_TILE,), jnp.int32), pltpu.SMEM((n_slots + MOE_TILE,), F32),
                        pltpu.SemaphoreType.DMA((2,))],
    )
    return pl.pallas_call(
        functools.partial(_moe_kernel, n_slots=n_slots, chunk=chunk),
        grid_spec=grid_spec,
        out_shape=jax.ShapeDtypeStruct((n_rows // chunk * out_rows, LANES), F32),
        compiler_params=_params(("arbitrary", "arbitrary")), name="moe",
    )(goff, cnt, tok, gw, f_moe, wgu, bgu, wd, bd), chunk


ATT_ROWS = 32
ATT_ROWS_UNROLL = 8


def _na_kernel(q_ref, k_ref, v_ref, kc_ref, vc_ref, bias_ref, o_ref, *, n_blocks, n_keys):
    blk = pl.program_id(2)
    key_row0 = jnp.clip(NA_QROWS * blk - NA_WIN_H // 2, 0, n_blocks * NA_QROWS - NA_KROWS)
    start = pl.multiple_of(key_row0 * GRID_W, GRID_W)
    q = q_ref[...]
    kl = k_ref[pl.ds(start, n_keys), :]
    vl = v_ref[pl.ds(start, n_keys), :]
    kc = kc_ref[...]
    vc = vc_ref[...]
    lane = lax.broadcasted_iota(jnp.int32, (1, LANES), 1)
    scale = NA_HEAD_DIM ** -0.5
    outs = []
    for a in range(2):
        sel = (lane < NA_HEAD_DIM) if a == 0 else (lane >= NA_HEAD_DIM)
        qa = jnp.where(sel, q, jnp.zeros_like(q)) * scale
        s_lat = _dot_nt(qa, kl) + bias_ref[0, a]
        s_ctx = _dot_nt(qa, kc)
        m = jnp.maximum(jnp.max(s_lat, axis=1, keepdims=True), jnp.max(s_ctx, axis=1, keepdims=True))
        e_lat = jnp.exp(s_lat - m)
        e_ctx = jnp.exp(s_ctx - m)
        den = jnp.sum(e_lat, axis=1, keepdims=True) + jnp.sum(e_ctx, axis=1, keepdims=True)
        outs.append((_dot(e_lat.astype(BF16), vl) + _dot(e_ctx.astype(BF16), vc)) / den)
    o_ref[...] = jnp.where(lane < NA_HEAD_DIM, outs[0], outs[1]).astype(BF16)


def _na_bias_table(rpb, rows):
    H = rpb.shape[0]
    qc = np.arange(GRID_W)
    col_start = np.clip(qc - NA_WIN_W // 2, 0, GRID_W - NA_WIN_W)
    col_in = (qc[None, :] >= col_start[:, None]) & (qc[None, :] < col_start[:, None] + NA_WIN_W)
    dx = np.clip(qc[None, :] - qc[:, None], 1 - NA_WIN_W, NA_WIN_W - 1) + NA_WIN_W - 1
    n_blocks = rows // NA_QROWS
    pick_y, masks = [], []
    for blk in (0, 1, n_blocks - 1):
        r = NA_QROWS * blk + np.arange(NA_QROWS)
        key_row0 = int(np.clip(NA_QROWS * blk - NA_WIN_H // 2, 0, rows - NA_KROWS))
        kr = key_row0 + np.arange(NA_KROWS)
        win0 = np.clip(r - NA_WIN_H // 2, 0, rows - NA_WIN_H)
        row_in = (kr[None, :] >= win0[:, None]) & (kr[None, :] < win0[:, None] + NA_WIN_H)
        dy = np.clip(kr[None, :] - r[:, None] + NA_WIN_H - 1, 0, 2 * NA_WIN_H - 2)
        pick_y.append(np.eye(2 * NA_WIN_H - 1, dtype=np.float32)[dy.reshape(-1)])
        masks.append(row_in[:, None, :, None] & col_in[None, :, None, :])
    pick_x = np.eye(2 * NA_WIN_W - 1, dtype=np.float32)[dx.reshape(-1)]
    bias = jnp.einsum('hyx,cay,bx->chab', rpb.astype(F32), jnp.asarray(np.stack(pick_y)), jnp.asarray(pick_x),
                      precision=lax.Precision.HIGHEST)
    bias = bias.reshape(3, H, NA_QROWS, NA_KROWS, GRID_W, GRID_W).transpose(0, 1, 2, 4, 3, 5)
    tabs = jnp.where(jnp.asarray(np.stack(masks))[:, None], bias, NEG_INF)
    tabs = tabs.reshape(3, H, NA_QROWS * GRID_W, NA_KROWS * GRID_W)
    return jnp.concatenate([tabs, jnp.full_like(tabs[:1], NEG_INF)], axis=0)


def _neighbourhood_attention(att, bias_tab, n_batch, n_lat, n_ctx):
    T = att.shape[0]
    qrows = NA_QROWS * GRID_W
    n_blocks = n_lat // qrows
    n_keys = NA_KROWS * GRID_W
    pairs = NA_WIDTH // LANES
    assert n_ctx == qrows
    lat_blocks = n_batch * n_blocks

    def q_map(b, hp, blk):
        return (jnp.where(blk < n_blocks, b * n_blocks + blk, lat_blocks + b), hp)

    def cls_map(b, hp, blk):
        cls = jnp.where(blk == 0, 0, jnp.where(blk == n_blocks - 1, 2, jnp.where(blk == n_blocks, 3, 1)))
        return (cls, hp, 0, 0)

    ctx_row = n_batch * n_lat // n_ctx
    return pl.pallas_call(
        functools.partial(_na_kernel, n_blocks=n_blocks, n_keys=n_keys),
        grid=(n_batch, pairs, n_blocks + 1),
        in_specs=[pl.BlockSpec((qrows, LANES), q_map),
                  pl.BlockSpec((n_lat, LANES), lambda b, hp, blk: (b, pairs + hp)),
                  pl.BlockSpec((n_lat, LANES), lambda b, hp, blk: (b, 2 * pairs + hp)),
                  pl.BlockSpec((n_ctx, LANES), lambda b, hp, blk: (ctx_row + b, pairs + hp)),
                  pl.BlockSpec((n_ctx, LANES), lambda b, hp, blk: (ctx_row + b, 2 * pairs + hp)),
                  pl.BlockSpec((1, 2, qrows, n_keys), cls_map)],
        out_specs=pl.BlockSpec((qrows, LANES), q_map),
        out_shape=jax.ShapeDtypeStruct((T, NA_WIDTH), BF16),
        compiler_params=_params(("arbitrary", "arbitrary", "arbitrary")), name="na_attention",
    )(att, att, att, att, att, bias_tab)


SWA_BLOCK = 128
SWA_BAND = 3 * SWA_BLOCK
def _swa_mask_table(n_lat):
    r = np.arange(SWA_BLOCK)[:, None]
    j = np.arange(SWA_BAND)[None, :]
    tabs = []
    n_blocks = n_lat // SWA_BLOCK
    for nb in (0, 1, n_blocks - 1):
        start = int(np.clip((nb - 1) * SWA_BLOCK, 0, n_lat - SWA_BAND))
        delta = nb * SWA_BLOCK + r - (start + j)
        tabs.append(np.where(np.abs(delta) <= SWA_WINDOW, 0.0, NEG_INF))
    tabs.append(np.full((SWA_BLOCK, SWA_BAND), NEG_INF))
    return jnp.asarray(np.stack(tabs), F32)


def _swa_kernel(sink_ref, q_ref, k_ref, v_ref, kc_ref, vc_ref, mask_ref, o_ref, s_ref, p_ref, den_ref, *, n_lat, group):
    kvh = pl.program_id(1)
    nb = pl.program_id(2)
    start = pl.multiple_of(jnp.clip((nb - 1) * SWA_BLOCK, 0, n_lat - SWA_BAND), SWA_BLOCK)
    kl = k_ref[pl.ds(start, SWA_BAND), :]
    vl = v_ref[pl.ds(start, SWA_BAND), :]
    q = q_ref[...]
    lane = lax.broadcasted_iota(jnp.int32, (1, LANES), 1)
    first_head = (lane & (SWA_HEAD_DIM - 1)) < SWA_HEAD_DIM // 2
    scale = SWA_HEAD_DIM ** -0.5
    parts = []
    for g in range(group):
        qp = q[:, (g // 2) * LANES:(g // 2 + 1) * LANES]
        sel = first_head if g % 2 == 0 else jnp.logical_not(first_head)
        parts.append(jnp.where(sel, qp, jnp.zeros_like(qp)) * scale)
    qs = jnp.concatenate(parts, axis=0)
    s_ref[:, :SWA_BAND] = _dot_nt(qs, kl)
    s_ref[:, SWA_BAND:] = _dot_nt(qs, kc_ref[...])

    def softmax_rows(i, carry):
        for u in range(ATT_ROWS_UNROLL):
            r0 = pl.multiple_of((i * ATT_ROWS_UNROLL + u) * ATT_ROWS, ATT_ROWS)
            rows = pl.ds(r0, ATT_ROWS)
            mrows = pl.ds(pl.multiple_of(r0 & (SWA_BLOCK - 1), ATT_ROWS), ATT_ROWS)
            s_lat = s_ref[rows, :SWA_BAND] + mask_ref[0, mrows, :]
            s_ctx = s_ref[rows, SWA_BAND:]
            blocks = ([s_lat[:, c * LANES:(c + 1) * LANES] for c in range(SWA_BAND // LANES)]
                      + [s_ctx[:, c * LANES:(c + 1) * LANES] for c in range(s_ctx.shape[1] // LANES)])
            sink = jnp.full((ATT_ROWS, 1), sink_ref[kvh * group + r0 // SWA_BLOCK], F32)
            m = jnp.maximum(jnp.max(functools.reduce(jnp.maximum, blocks), axis=1, keepdims=True), sink)
            es = [jnp.exp(b - m) for b in blocks]
            den = jnp.sum(functools.reduce(jnp.add, es), axis=1, keepdims=True) + jnp.exp(sink - m)
            for c, e in enumerate(es):
                p_ref[rows, c * LANES:(c + 1) * LANES] = e.astype(BF16)
            den_ref[rows, :] = jnp.broadcast_to(den, (ATT_ROWS, LANES))
        return carry

    lax.fori_loop(0, group * SWA_BLOCK // (ATT_ROWS * ATT_ROWS_UNROLL), softmax_rows, 0)
    o = (_dot(p_ref[:, :SWA_BAND], vl) + _dot(p_ref[:, SWA_BAND:], vc_ref[...])) / den_ref[...]
    for p in range(group // 2):
        o_a = o[(2 * p) * SWA_BLOCK:(2 * p + 1) * SWA_BLOCK]
        o_b = o[(2 * p + 1) * SWA_BLOCK:(2 * p + 2) * SWA_BLOCK]
        o_ref[:, p * LANES:(p + 1) * LANES] = jnp.where(lane < SWA_HEAD_DIM, o_a, o_b).astype(BF16)


def _window_attention(att, sink, n_batch, n_lat, n_ctx, with_ctx):
    T = att.shape[0]
    group = SWA_Q_HEADS // SWA_KV_HEADS
    qw = group * SWA_HEAD_DIM
    n_lat_blocks = n_lat // SWA_BLOCK
    n_ctx_blocks = n_ctx // SWA_BLOCK if with_ctx else 0
    k_col = SWA_Q_HEADS * SWA_HEAD_DIM // LANES
    v_col = k_col + SWA_KV_HEADS
    ctx_row = n_batch * n_lat // n_ctx

    def q_map(b, kvh, nb):
        lat = b * n_lat_blocks + nb
        ctx = n_batch * n_lat_blocks + b * (n_ctx // SWA_BLOCK) + (nb - n_lat_blocks)
        return (jnp.where(nb < n_lat_blocks, lat, ctx), kvh)

    def cls_map(b, kvh, nb):
        cls = jnp.where(nb == 0, 0, jnp.where(nb == n_lat_blocks - 1, 2, jnp.where(nb >= n_lat_blocks, 3, 1)))
        return (cls, 0, 0)

    grid_spec = pltpu.PrefetchScalarGridSpec(
        num_scalar_prefetch=0,
        grid=(n_batch, SWA_KV_HEADS, n_lat_blocks + n_ctx_blocks),
        in_specs=[pl.BlockSpec(memory_space=pltpu.SMEM),
                  pl.BlockSpec((SWA_BLOCK, qw), q_map),
                  pl.BlockSpec((n_lat, LANES), lambda b, kvh, nb: (b, k_col + kvh)),
                  pl.BlockSpec((n_lat, LANES), lambda b, kvh, nb: (b, v_col + kvh)),
                  pl.BlockSpec((n_ctx, LANES), lambda b, kvh, nb: (ctx_row + b, k_col + kvh)),
                  pl.BlockSpec((n_ctx, LANES), lambda b, kvh, nb: (ctx_row + b, v_col + kvh)),
                  pl.BlockSpec((1, SWA_BLOCK, SWA_BAND), cls_map)],
        out_specs=pl.BlockSpec((SWA_BLOCK, qw), q_map),
        scratch_shapes=[pltpu.VMEM((group * SWA_BLOCK, SWA_BAND + n_ctx), F32),
                        pltpu.VMEM((group * SWA_BLOCK, SWA_BAND + n_ctx), BF16),
                        pltpu.VMEM((group * SWA_BLOCK, LANES), F32)],
    )
    return pl.pallas_call(
        functools.partial(_swa_kernel, n_lat=n_lat, group=group),
        grid_spec=grid_spec,
        out_shape=jax.ShapeDtypeStruct((T if with_ctx else n_batch * n_lat, SWA_Q_HEADS * SWA_HEAD_DIM), BF16),
        compiler_params=_params(("arbitrary", "arbitrary", "arbitrary")), name="window_attention",
    )(sink.astype(F32), att, att, att, att, att, _swa_mask_table(n_lat))


DN_PAIR = 2 * DN_CHUNK
DN_GROUP = 4
DN_NQ = 2 * (DN_HEAD_DIM + DN_CHUNK)


def _bdot(a, b):
    return jnp.einsum('gik,gkj->gij', a, b, preferred_element_type=F32)


def _bdot_nt(a, b):
    return jnp.einsum('gik,gjk->gij', a, b, preferred_element_type=F32)


def _split2(x):
    hi = x.astype(BF16)
    return hi, (x - hi.astype(F32)).astype(BF16)


def _softplus(x):
    return jnp.maximum(x, 0.0) + jnp.log(1.0 + jnp.exp(-jnp.abs(x)))


def _pack_dot(a0, b0, a1, b1):
    z = jnp.zeros_like(b0)
    rhs = jnp.concatenate([jnp.concatenate([b0, z], axis=2), jnp.concatenate([z, b1], axis=2)], axis=1)
    r = _bdot(jnp.concatenate([a0, a1], axis=2), rhs)
    return r[:, :, :DN_PAIR], r[:, :, DN_PAIR:]


def _pack_dot3(a0, b0, a1, b1):
    (a0h, a0l), (b0h, b0l), (a1h, a1l), (b1h, b1l) = _split2(a0), _split2(b0), _split2(a1), _split2(b1)
    hh = _pack_dot(a0h, b0h, a1h, b1h)
    hl = _pack_dot(a0h, b0l, a1h, b1l)
    lh = _pack_dot(a0l, b0h, a1l, b1h)
    return hh[0] + (hl[0] + lh[0]), hh[1] + (hl[1] + lh[1])


def _dn_prepare(q, k, v, gcs, betas):
    G = q.shape[0]
    ii = lax.broadcasted_iota(jnp.int32, (DN_PAIR, DN_PAIR), 0)
    jj = lax.broadcasted_iota(jnp.int32, (DN_PAIR, DN_PAIR), 1)
    same = (ii >> 6) == (jj >> 6)
    incl = (same & (ii >= jj), same & (ii <= jj))
    strict = (same & (ii > jj), same & (ii < jj))
    eye = (ii == jj).astype(F32)
    eye16 = jnp.broadcast_to(eye.astype(BF16), (G, DN_PAIR, DN_PAIR))
    lane0 = jnp.broadcast_to((jj == 0).astype(BF16), (G, DN_PAIR, DN_PAIR))
    g_rows = sum(_bdot_nt(lane0, jnp.concatenate([pf, pb], axis=1)) for pf, pb in zip(_split3(gcs[0]), _split3(gcs[1])))
    g_row = (g_rows[:, :, :DN_PAIR], g_rows[:, :, DN_PAIR:])
    decay = [jnp.where(incl[d], jnp.exp(jnp.where(incl[d], gcs[d] - g_row[d], 0.0)), 0.0) for d in range(2)]
    k16 = k.astype(BF16)
    k_t = _bdot_nt(eye16, k16).astype(BF16)
    kb = [k * betas[d] for d in range(2)]
    vb = [v * betas[d] for d in range(2)]
    kk = _pack_dot(kb[0].astype(BF16), k_t, kb[1].astype(BF16), k_t)
    m = [jnp.where(strict[d], kk[d] * decay[d], 0.0) for d in range(2)]
    inv = [eye - m[0], eye - m[1]]
    p = m
    for _ in range(5):
        p = _pack_dot3(p[0], p[0], p[1], p[1])
        upd = _pack_dot3(inv[0], p[0], inv[1], p[1])
        inv = [inv[0] + upd[0], inv[1] + upd[1]]
    qk = _bdot(q.astype(BF16), k_t)
    first_chunk = lax.broadcasted_iota(jnp.int32, (DN_PAIR, 1), 0) < DN_CHUNK
    uws, a_intras, qgs, kgs, egls = [], [], [], [], []
    for d in range(2):
        gc = gcs[d]
        eg = jnp.exp(gc)
        uws.append(_bdot(inv[d].astype(BF16), jnp.concatenate([kb[d] * eg, vb[d]], axis=2).astype(BF16)).astype(BF16))
        a_intras.append(jnp.where(incl[d], qk * decay[d], 0.0).astype(BF16))
        lo, hi = (DN_CHUNK - 1, DN_PAIR - 1) if d == 0 else (0, DN_CHUNK)
        g_last = jnp.where(first_chunk, gc[:, lo:lo + 1, :], gc[:, hi:hi + 1, :])
        kgs.append((k * jnp.exp(g_last - gc)).astype(BF16))
        qgs.append(q * eg)
        egls.append(jnp.exp(g_last))
    kg_ts = _bdot_nt(eye16, jnp.concatenate(kgs, axis=1)).astype(BF16)
    zero = jnp.zeros((G, DN_PAIR, 2 * LANES), BF16)
    res = []
    for d in range(2):
        wu = uws[d]
        au = _bdot(a_intras[d], wu)
        q_eff = (qgs[d] - au[:, :, :LANES]).astype(BF16)
        wu_halves = jnp.concatenate([jnp.where(first_chunk, wu, zero), jnp.where(first_chunk, zero, wu)], axis=2)
        nb = _bdot(kg_ts[:, :, d * DN_PAIR:(d + 1) * DN_PAIR], wu_halves)
        nq = jnp.concatenate([(-nb[:, :, :LANES]).astype(BF16), q_eff[:, :DN_CHUNK],
                              (-nb[:, :, 2 * LANES:3 * LANES]).astype(BF16), q_eff[:, DN_CHUNK:]], axis=1)
        bb = jnp.concatenate([nb[:, :, LANES:2 * LANES], nb[:, :, 3 * LANES:]], axis=1)
        res.append((nq, bb, au[:, :, LANES:], egls[d]))
    return res


def _dn_sequence(q_ref, k_ref, v_ref, z_ref, ab_ref, cw_refs, gpar_ref, nw_ref, o_ref, s0, head, T, sc):
    (qn, kn, vn, gcs, bts, nqs, bss, egls, outs) = sc
    row = lax.broadcasted_iota(jnp.int32, (T, 1), 0)

    def conv_silu(x, w_ref):
        acc = x * w_ref[DN_CONV // 2:DN_CONV // 2 + 1, :]
        for j in range(DN_CONV):
            d = j - DN_CONV // 2
            if d == 0:
                continue
            shifted = pltpu.roll(x, (-d) % T, axis=0)
            ok = (row + d >= 0) & (row + d < T)
            acc = acc + jnp.where(ok, shifted, 0.0) * w_ref[j:j + 1, :]
        return acc * _sigmoid(acc)

    q = conv_silu(q_ref[...], cw_refs[0])
    k = conv_silu(k_ref[...], cw_refs[1])
    v = conv_silu(v_ref[...], cw_refs[2])
    q = q * lax.rsqrt(jnp.sum(q * q, axis=1, keepdims=True) + RMS_EPS) * DN_HEAD_DIM ** -0.5
    k = k * lax.rsqrt(jnp.sum(k * k, axis=1, keepdims=True) + RMS_EPS)
    qn[pl.ds(0, T), :] = q
    kn[pl.ds(0, T), :] = k
    vn[pl.ds(0, T), :] = v

    ab = ab_ref[...]
    g_all = -jnp.exp(gpar_ref[0:1, :]) * _softplus(ab + gpar_ref[1:2, :])
    b_all = _sigmoid(ab)
    lane = lax.broadcasted_iota(jnp.int32, (1, LANES), 1)
    src = jnp.where(lane < 2 * DN_HEADS, g_all, b_all)
    krow = lax.broadcasted_iota(jnp.int32, (3 * LANES, 4 * LANES), 0) & (LANES - 1)
    kcol = lax.broadcasted_iota(jnp.int32, (3 * LANES, 4 * LANES), 1) >> 7
    pick = (krow == kcol * DN_HEADS + head).astype(BF16)
    cols = _dot(jnp.concatenate(_split3(src), axis=1), pick)

    pos = row & (DN_CHUNK - 1)
    for d in range(2):
        g = cols[:, d * LANES:(d + 1) * LANES]
        for sh in (1, 2, 4, 8, 16, 32):
            if d == 0:
                g = g + jnp.where(pos >= sh, pltpu.roll(g, sh, axis=0), 0.0)
            else:
                g = g + jnp.where(pos < DN_CHUNK - sh, pltpu.roll(g, T - sh, axis=0), 0.0)
        gcs[d][pl.ds(0, T), :] = g
        bts[d][pl.ds(0, T), :] = cols[:, (2 + d) * LANES:(3 + d) * LANES]

    n_pairs = T // DN_PAIR
    G = min(DN_GROUP, n_pairs)
    R = G * DN_PAIR

    def prepare(gi, carry):
        rows = pl.ds(pl.multiple_of(gi * R, R), R)
        shape = (G, DN_PAIR, LANES)
        qq = qn[rows, :].reshape(shape)
        kk = kn[rows, :].reshape(shape)
        vv = vn[rows, :].reshape(shape)
        res = _dn_prepare(qq, kk, vv, [gcs[d][rows, :].reshape(shape) for d in range(2)],
                          [bts[d][rows, :].reshape(shape) for d in range(2)])
        for d in range(2):
            nq, bb, o_const, egl = res[d]
            nqs[d][pl.ds(pl.multiple_of(gi * G * DN_NQ, G * DN_NQ), G * DN_NQ), :] = nq.reshape(G * DN_NQ, LANES)
            bss[d][pl.ds(pl.multiple_of(gi * 2 * R, 2 * R), 2 * R), :] = bb.reshape(2 * R, LANES)
            outs[d][rows, :] = o_const.reshape(R, LANES)
            egls[d][rows, :] = egl.reshape(R, LANES)
        return carry

    lax.fori_loop(0, n_pairs // G, prepare, 0)

    def chunk_step(S, pair, half, d):
        off = pl.multiple_of(pair * DN_PAIR + half * DN_CHUNK, DN_CHUNK)
        rows = pl.ds(off, DN_CHUNK)
        lhs = nqs[d][pl.ds(pl.multiple_of(pair * DN_NQ + half * (DN_NQ // 2), DN_NQ // 2), DN_NQ // 2), :]
        r = _dot(lhs, S.astype(BF16))
        outs[d][rows, :] = outs[d][rows, :] + r[DN_HEAD_DIM:]
        decay = jnp.broadcast_to(egls[d][pl.ds(off, 1), :], (DN_HEAD_DIM, LANES))
        add = bss[d][pl.ds(pl.multiple_of(pair * 2 * DN_HEAD_DIM + half * DN_HEAD_DIM, DN_HEAD_DIM), DN_HEAD_DIM), :]
        return S * decay + (r[:DN_HEAD_DIM] + add)

    def scan(p, carry):
        s_f, s_b = carry
        pb = n_pairs - 1 - p
        s_f = chunk_step(s_f, p, 0, 0)
        s_b = chunk_step(s_b, pb, 1, 1)
        s_f = chunk_step(s_f, p, 1, 0)
        s_b = chunk_step(s_b, pb, 0, 1)
        return s_f, s_b

    s_f, s_b = lax.fori_loop(0, n_pairs, scan, s0)

    o = outs[0][pl.ds(0, T), :] + outs[1][pl.ds(0, T), :]
    o = o * lax.rsqrt(jnp.mean(o * o, axis=1, keepdims=True) + RMS_EPS) * nw_ref[...]
    z = z_ref[...]
    o_ref[...] = (o * (z * _sigmoid(z))).astype(BF16)
    return s_f, s_b


def _dn_kernel(ql, kl, vl, zl, abl, qc, kc, vc, zc, abc, cwq, cwk, cwv, gpar, nw, ol, oc, *scratch, n_lat, n_ctx):
    head = pl.program_id(1)
    it = iter(scratch)
    sc = [next(it) if i < 3 else (next(it), next(it)) for i in range(9)]
    zero = jnp.zeros((DN_HEAD_DIM, DN_HEAD_DIM), F32)
    s_ctx = _dn_sequence(qc, kc, vc, zc, abc, (cwq, cwk, cwv), gpar, nw, oc, (zero, zero), head, n_ctx, sc)
    _dn_sequence(ql, kl, vl, zl, abl, (cwq, cwk, cwv), gpar, nw, ol, s_ctx, head, n_lat, sc)


def _delta_net(dn, conv_w, a_log, dt_bias, norm_w, n_batch, n_lat, n_ctx):
    H = DN_HEADS
    ctx_row = n_batch * n_lat // n_ctx
    ab_col = 4 * H

    def lat(off):
        return pl.BlockSpec((n_lat, LANES), lambda b, h: (b, off + h))

    def ctx(off):
        return pl.BlockSpec((n_ctx, LANES), lambda b, h: (ctx_row + b, off + h))

    cw = jnp.zeros((SUBLANES, 3 * DN_WIDTH), F32).at[:DN_CONV].set(conv_w.astype(F32))
    gpar = jnp.zeros((SUBLANES, LANES), F32)
    gpar = gpar.at[0, :2 * H].set(a_log.reshape(-1).astype(F32)).at[1, :2 * H].set(dt_bias.reshape(-1).astype(F32))
    in_specs = [lat(0), lat(H), lat(2 * H), lat(3 * H), pl.BlockSpec((n_lat, LANES), lambda b, h: (b, ab_col)),
                ctx(0), ctx(H), ctx(2 * H), ctx(3 * H), pl.BlockSpec((n_ctx, LANES), lambda b, h: (ctx_row + b, ab_col)),
                pl.BlockSpec((SUBLANES, LANES), lambda b, h: (0, h)),
                pl.BlockSpec((SUBLANES, LANES), lambda b, h: (0, H + h)),
                pl.BlockSpec((SUBLANES, LANES), lambda b, h: (0, 2 * H + h)),
                pl.BlockSpec((SUBLANES, LANES), lambda b, h: (0, 0)),
                pl.BlockSpec((1, LANES), lambda b, h: (0, 0))]
    f32s = lambda rows=n_lat: pltpu.VMEM((rows, LANES), F32)
    n_pairs = n_lat // DN_PAIR
    scratch = [f32s(), f32s(), f32s()]
    scratch += [f32s() for _ in range(4)]
    scratch += [pltpu.VMEM((n_pairs * DN_NQ, LANES), BF16) for _ in range(2)]
    scratch += [f32s(2 * n_pairs * DN_HEAD_DIM) for _ in range(2)]
    scratch += [f32s() for _ in range(4)]
    return pl.pallas_call(
        functools.partial(_dn_kernel, n_lat=n_lat, n_ctx=n_ctx),
        grid=(n_batch, H),
        in_specs=in_specs,
        out_specs=[pl.BlockSpec((n_lat, LANES), lambda b, h: (b, h)),
                   pl.BlockSpec((n_ctx, LANES), lambda b, h: (b, h))],
        out_shape=[jax.ShapeDtypeStruct((n_batch * n_lat, DN_WIDTH), BF16),
                   jax.ShapeDtypeStruct((n_batch * n_ctx, DN_WIDTH), BF16)],
        scratch_shapes=scratch,
        compiler_params=_params(("arbitrary", "arbitrary")), name="delta_net",
    )(dn, dn, dn, dn, dn, dn, dn, dn, dn, dn, cw, cw, cw, gpar, norm_w.reshape(1, -1).astype(F32))


def _rope_tables(n_tokens):
    t = jnp.arange(n_tokens)
    n_freq = SWA_HEAD_DIM // 4
    inv = ROPE_THETA ** (-jnp.arange(n_freq, dtype=F32) / n_freq)
    row = (t // GRID_W).astype(F32)[:, None]
    col = (t % GRID_W).astype(F32)[:, None]
    ang = jnp.concatenate([row * inv, col * inv], -1)
    cos, sin = jnp.cos(ang), jnp.sin(ang)
    cos_t = jnp.concatenate([cos, cos, cos, cos], -1)
    sin_t = jnp.concatenate([-sin, -sin, sin, sin], -1)
    ident = jnp.ones((TOK_TILE, LANES), F32)
    return (jnp.concatenate([cos_t, ident], 0), jnp.concatenate([sin_t, jnp.zeros_like(ident)], 0))


def _swa_weight(w_in):
    D = w_in.shape[0]
    qw = SWA_Q_HEADS * SWA_HEAD_DIM
    kw = SWA_KV_HEADS * SWA_HEAD_DIM
    half = SWA_HEAD_DIM // 2
    wq = w_in[:, :qw].reshape(D, SWA_Q_HEADS // 2, 2, 2, half)
    wq = jnp.transpose(wq, (0, 1, 3, 2, 4)).reshape(D, qw)
    wk = w_in[:, qw:qw + kw].reshape(D, SWA_KV_HEADS, 2, 1, half)
    wk = jnp.broadcast_to(wk, (D, SWA_KV_HEADS, 2, 2, half)).reshape(D, 2 * kw)
    wv = w_in[:, qw + kw:].reshape(D, SWA_KV_HEADS, 1, SWA_HEAD_DIM)
    wv = jnp.broadcast_to(wv, (D, SWA_KV_HEADS, 2, SWA_HEAD_DIM)).reshape(D, 2 * kw)
    return jnp.concatenate([wq, wk, wv], axis=1).astype(BF16)


def kernel(x, c, ctx, c_ctx, w_mod, b_mod, ln_g, ln_b, w_in_ab, na_rpb, dn_conv, dn_a_log, dn_dt_bias, dn_norm_w,
           w_out_ab, w_in_c, swa_sink, w_out_c, w_router, b_router, w_gu, b_gu, w_down, b_down):
    B, N, D = x.shape
    L = ctx.shape[1]
    depth = w_mod.shape[0]
    alpha = (2 * depth) ** 0.25
    assert D == D_MODEL and N % TOK_TILE == 0 and L == TOK_TILE
    geom = (B * N // TOK_TILE, N // TOK_TILE, B)

    mod_rows = -(-(B + 1) // 16) * 16
    cc = jnp.zeros((mod_rows, D), F32).at[:B].set(c).at[B].set(c_ctx)
    mod_all = _modulation(cc, w_mod, b_mod)
    cos_t, sin_t = _rope_tables(N)
    n_att_ab = 3 * NA_WIDTH
    ab_main = n_att_ab + 4 * DN_WIDTH
    wr = jnp.zeros((depth, D, LANES), BF16).at[:, :, :N_EXPERTS].set(w_router.astype(BF16))
    br = jnp.full((depth, 1, LANES), NEG_INF, F32).at[:, 0, :N_EXPERTS].set(b_router.astype(F32))
    wgu16, wd16 = w_gu.astype(BF16), w_down.astype(BF16)
    bgu4, bd4 = b_gu[:, :, None, :].astype(F32), b_down[:, :, None, :].astype(F32)

    h = jnp.concatenate([x.reshape(B * N, D), ctx.reshape(B * L, D)], axis=0)
    moe_prev = mod_prev = ln_prev = None
    out = None
    for layer in range(depth):
        last = layer == depth - 1
        i = layer // 2
        mod = mod_all[layer].reshape(mod_rows, 1, -1)
        if layer % 2 == 0:
            w = w_in_ab[i]
            w = jnp.concatenate([w[:, :ab_main], jnp.pad(w[:, ab_main:], ((0, 0), (0, LANES - (w.shape[1] - ab_main))))], 1)
            h, att, dn = _proj_in(h, moe_prev, mod_prev, ln_prev, mod, w.astype(BF16), geom, n_att=n_att_ab, alpha=alpha)
            o_a = _neighbourhood_attention(att, _na_bias_table(na_rpb[i], N // GRID_W), B, N, L)
            o_bl, o_bc = _delta_net(dn, dn_conv[i], dn_a_log[i], dn_dt_bias[i], dn_norm_w[i], B, N, L)
            xs = [o_a, jnp.concatenate([o_bl, o_bc], axis=0)]
            wo = w_out_ab[i].astype(BF16)
            ws = [wo[:NA_WIDTH], wo[NA_WIDTH:]]
        else:
            w = _swa_weight(w_in_c[i])
            n_rope = (SWA_Q_HEADS + 2 * SWA_KV_HEADS) * SWA_HEAD_DIM // LANES
            h, att, _ = _proj_in(h, moe_prev, mod_prev, ln_prev, mod, w, geom, n_att=w.shape[1],
                                 rope=(cos_t, sin_t, n_rope), alpha=alpha)
            xs = [_window_attention(att, swa_sink[i], B, N, L, with_ctx=not last)]
            ws = [w_out_c[i].astype(BF16)]
        n_rows = B * N if last else B * (N + L)
        ln1 = jnp.stack([ln_g[layer, 0], ln_b[layer, 0]])
        ln2 = jnp.stack([ln_g[layer, 1], ln_b[layer, 1]])
        h1, f_moe, idx, gates = _proj_out(xs, ws, h, mod, ln1, wr[layer], br[layer], geom, n_rows, alpha=alpha)
        y_moe = _moe(f_moe, idx, gates, wgu16, bgu4, wd16, bd4, layer)
        if last:
            out = _final_combine(h1, y_moe, mod, ln2, geom, alpha=alpha).reshape(B, N, D)
        else:
            h, moe_prev, mod_prev, ln_prev = h1, y_moe, mod, ln2
    return out
```

```python
import functools

import jax
import jax.numpy as jnp
import numpy as np
from jax import lax
from jax.experimental import pallas as pl
from jax.experimental.pallas import tpu as pltpu

F32 = jnp.float32
BF16 = jnp.bfloat16

D_MODEL = 1024
GRID_W = 64
NA_HEADS = 8
NA_HEAD_DIM = 64
NA_WIN_H = 8
NA_WIN_W = 16
DN_HEADS = 4
DN_HEAD_DIM = 128
DN_CONV = 5
DN_CHUNK = 64
SWA_Q_HEADS = 16
SWA_KV_HEADS = 2
SWA_HEAD_DIM = 64
SWA_WINDOW = 128
ROPE_THETA = 10000.0
N_EXPERTS = 32
TOP_K = 4
D_EXPERT = 1024
SWIGLU_LIMIT = 7.0
SWIGLU_ALPHA = 1.702
LN_EPS = 1e-5
RMS_EPS = 1e-6
NEG_INF = -1e30

LANES = 128
SUBLANES = 8
TOK_TILE = 256
MOE_TILE = 128
MOE_CHUNK_MAX = 4608
VMEM_LIMIT = 60 * 1024 * 1024

NA_WIDTH = NA_HEADS * NA_HEAD_DIM
DN_WIDTH = DN_HEADS * DN_HEAD_DIM
NA_QROWS = 4
NA_KROWS = 12


def _params(sem):
    return pltpu.CompilerParams(dimension_semantics=sem, vmem_limit_bytes=VMEM_LIMIT)


def _sigmoid(x):
    return 1.0 / (1.0 + jnp.exp(-x))


def _dot(a, b):
    return jnp.dot(a, b, preferred_element_type=F32)


def _dot_nt(a, b):
    return lax.dot_general(a, b, (((1,), (1,)), ((), ())), preferred_element_type=F32)


def _moe_layout_load(ref, rows):
    return jnp.concatenate([ref[pl.ds(s, rows, stride=SUBLANES), :] for s in range(SUBLANES)], axis=1)


def _moe_layout_store(ref, val, rows):
    for s in range(SUBLANES):
        ref[pl.ds(s, rows, stride=SUBLANES), :] = val[:, s * LANES:(s + 1) * LANES]


def _layer_norm(y, g, b):
    mu = jnp.mean(y, axis=-1, keepdims=True)
    d = y - mu
    var = jnp.mean(d * d, axis=-1, keepdims=True)
    return d * lax.rsqrt(var + LN_EPS) * g + b


def _mod_kernel(s_ref, w_ref, b_ref, o_ref):
    s = s_ref[...]
    s = s * _sigmoid(s)
    o_ref[0] = _dot(s.astype(BF16), w_ref[0].astype(BF16)) + b_ref[0]


def _modulation(cc, w_mod, b_mod):
    depth = w_mod.shape[0]
    rows = cc.shape[0]
    nblk = w_mod.shape[2] // D_MODEL
    return pl.pallas_call(
        _mod_kernel,
        grid=(depth, nblk),
        in_specs=[pl.BlockSpec((rows, D_MODEL), lambda l, j: (0, 0)),
                  pl.BlockSpec((1, D_MODEL, D_MODEL), lambda l, j: (l, 0, j)),
                  pl.BlockSpec((1, 1, D_MODEL), lambda l, j: (l, 0, j))],
        out_specs=pl.BlockSpec((1, rows, D_MODEL), lambda l, j: (l, 0, j)),
        out_shape=jax.ShapeDtypeStruct((depth, rows, nblk * D_MODEL), F32),
        compiler_params=_params(("arbitrary", "arbitrary")),
        name="modulation",
    )(cc, w_mod, b_mod.reshape(depth, 1, -1))


def _pin_kernel(*refs, first, n_rope, n_att, alpha):
    refs = list(refs)
    h_ref = refs.pop(0)
    if not first:
        moe_ref, g2_ref, lg_ref, lb_ref = refs[:4]
        refs = refs[4:]
    sc_ref, sh_ref, w_ref = refs[:3]
    refs = refs[3:]
    if n_rope:
        cos_ref, sin_ref = refs[:2]
        refs = refs[2:]
    if not first:
        hout_ref = refs.pop(0)
    att_ref = refs.pop(0)
    dn_ref = refs.pop(0) if refs else None

    h = h_ref[...]
    if not first:
        m = _moe_layout_load(moe_ref, TOK_TILE)
        h = _layer_norm(alpha * h + g2_ref[0] * m, lg_ref[...], lb_ref[...])
        hout_ref[...] = h
    a = h * (1.0 + sc_ref[0]) + sh_ref[0]
    acc = _dot(a.astype(BF16), w_ref[...])
    if n_rope:
        cos = cos_ref[...]
        sin = sin_ref[...]
        for g in range(n_att // LANES):
            blk = acc[:, g * LANES:(g + 1) * LANES]
            if g < n_rope:
                blk = blk * cos + pltpu.roll(blk, LANES // 2, axis=1) * sin
            att_ref[:, g * LANES:(g + 1) * LANES] = blk.astype(BF16)
    else:
        att_ref[...] = acc[:, :n_att].astype(BF16)
    if dn_ref is not None:
        dn_ref[...] = acc[:, n_att:]


def _mod_spec(k, n_lat_tiles, lat_tiles_per_batch, n_batch):
    def imap(i):
        return (jnp.where(i < n_lat_tiles, i // lat_tiles_per_batch, n_batch), 0, k)
    return pl.BlockSpec((1, 1, D_MODEL), imap)


def _proj_in(h, moe_prev, mod_prev, ln_prev, mod, w, geom, *, n_att, rope=None, alpha=1.0):
    T = h.shape[0]
    n_tiles = T // TOK_TILE
    n_lat_tiles, per_batch, n_batch = geom
    first = moe_prev is None
    ncols = w.shape[1]
    row = pl.BlockSpec((TOK_TILE, D_MODEL), lambda i: (i, 0))
    vec = pl.BlockSpec((1, D_MODEL), lambda i: (0, 0))
    ms = functools.partial(_mod_spec, n_lat_tiles=n_lat_tiles, lat_tiles_per_batch=per_batch, n_batch=n_batch)
    args, specs = [h], [row]
    if not first:
        moe_arr, moe_chunk = moe_prev
        args += [moe_arr, mod_prev, ln_prev[0:1], ln_prev[1:2]]
        specs += [pl.BlockSpec((TOK_TILE * SUBLANES, LANES), lambda i: (_moe_block(i, moe_chunk), 0)), ms(5), vec, vec]
    args += [mod, mod, w]
    specs += [ms(1), ms(0), pl.BlockSpec((D_MODEL, ncols), lambda i: (0, 0))]
    n_rope = 0
    if rope is not None:
        cos_t, sin_t, n_rope = rope
        tab = pl.BlockSpec((TOK_TILE, LANES), lambda i: (jnp.where(i < n_lat_tiles, i % per_batch, per_batch), 0))
        args += [cos_t, sin_t]
        specs += [tab, tab]
    out_shapes, out_specs = [], []
    if not first:
        out_shapes.append(jax.ShapeDtypeStruct((T, D_MODEL), F32))
        out_specs.append(row)
    out_shapes.append(jax.ShapeDtypeStruct((T, n_att), BF16))
    out_specs.append(pl.BlockSpec((TOK_TILE, n_att), lambda i: (i, 0)))
    if ncols > n_att:
        out_shapes.append(jax.ShapeDtypeStruct((T, ncols - n_att), F32))
        out_specs.append(pl.BlockSpec((TOK_TILE, ncols - n_att), lambda i: (i, 0)))
    outs = pl.pallas_call(
        functools.partial(_pin_kernel, first=first, n_rope=n_rope, n_att=n_att, alpha=alpha),
        grid=(n_tiles,), in_specs=specs, out_specs=out_specs, out_shape=out_shapes,
        compiler_params=_params(("arbitrary",)), name="proj_in",
    )(*args)
    outs = list(outs)
    h_new = h if first else outs.pop(0)
    att = outs.pop(0)
    dn = outs.pop(0) if outs else None
    return h_new, att, dn


def _final_kernel(h_ref, moe_ref, g2_ref, lg_ref, lb_ref, o_ref, *, alpha):
    m = _moe_layout_load(moe_ref, TOK_TILE)
    o_ref[...] = _layer_norm(alpha * h_ref[...] + g2_ref[0] * m, lg_ref[...], lb_ref[...])


def _final_combine(h, moe, mod, ln, geom, *, alpha):
    T = h.shape[0]
    n_lat_tiles, per_batch, n_batch = geom
    moe, moe_chunk = moe
    row = pl.BlockSpec((TOK_TILE, D_MODEL), lambda i: (i, 0))
    vec = pl.BlockSpec((1, D_MODEL), lambda i: (0, 0))
    return pl.pallas_call(
        functools.partial(_final_kernel, alpha=alpha),
        grid=(T // TOK_TILE,),
        in_specs=[row, pl.BlockSpec((TOK_TILE * SUBLANES, LANES), lambda i: (_moe_block(i, moe_chunk), 0)),
                  _mod_spec(5, n_lat_tiles, per_batch, n_batch), vec, vec],
        out_specs=row, out_shape=jax.ShapeDtypeStruct((T, D_MODEL), F32),
        compiler_params=_params(("arbitrary",)), name="final_combine",
    )(h, moe, mod, ln[0:1], ln[1:2])


def _pout_kernel(*refs, n_in, alpha):
    xs = refs[:n_in]
    ws = refs[n_in:2 * n_in]
    (h_ref, g1_ref, lg_ref, lb_ref, sc_ref, sh_ref, wr_ref, br_ref,
     hout_ref, f_ref, idx_ref, gate_ref) = refs[2 * n_in:]
    m = _dot(xs[0][...], ws[0][...])
    for x_ref, w_ref in zip(xs[1:], ws[1:]):
        m = m + _dot(x_ref[...], w_ref[...])
    h1 = _layer_norm(alpha * h_ref[...] + g1_ref[0] * m, lg_ref[...], lb_ref[...])
    hout_ref[...] = h1
    f = h1 * (1.0 + sc_ref[0]) + sh_ref[0]
    _moe_layout_store(f_ref, f, TOK_TILE)
    logits = _dot(f.astype(BF16), wr_ref[...]) + br_ref[...]
    lane = lax.broadcasted_iota(jnp.int32, logits.shape, 1)
    work = logits
    vals, idxs = [], []
    for _ in range(TOP_K):
        v = jnp.max(work, axis=1, keepdims=True)
        ix = jnp.min(jnp.where(work == v, lane, LANES), axis=1, keepdims=True)
        vals.append(v)
        idxs.append(ix)
        work = jnp.where(lane == ix, -jnp.inf, work)
    es = [jnp.exp(v - vals[0]) for v in vals]
    den = es[0]
    for e in es[1:]:
        den = den + e
    idx_out = jnp.zeros(logits.shape, jnp.int32)
    gate_out = jnp.zeros(logits.shape, F32)
    for k in range(TOP_K):
        idx_out = jnp.where(lane == k, idxs[k], idx_out)
        gate_out = jnp.where(lane == k, es[k] / den, gate_out)
    idx_ref[...] = idx_out
    gate_ref[...] = gate_out


def _proj_out(xs, ws, h, mod, ln, w_router, b_router, geom, n_rows, *, alpha):
    n_lat_tiles, per_batch, n_batch = geom
    row = pl.BlockSpec((TOK_TILE, D_MODEL), lambda i: (i, 0))
    vec = pl.BlockSpec((1, D_MODEL), lambda i: (0, 0))
    lane_row = pl.BlockSpec((TOK_TILE, LANES), lambda i: (i, 0))
    ms = functools.partial(_mod_spec, n_lat_tiles=n_lat_tiles, lat_tiles_per_batch=per_batch, n_batch=n_batch)
    specs = [pl.BlockSpec((TOK_TILE, x.shape[1]), lambda i: (i, 0)) for x in xs]
    specs += [pl.BlockSpec(w.shape, lambda i: (0, 0)) for w in ws]
    specs += [row, ms(2), vec, vec, ms(4), ms(3),
              pl.BlockSpec((D_MODEL, LANES), lambda i: (0, 0)), pl.BlockSpec((1, LANES), lambda i: (0, 0))]
    return pl.pallas_call(
        functools.partial(_pout_kernel, n_in=len(xs), alpha=alpha),
        grid=(n_rows // TOK_TILE,), in_specs=specs,
        out_specs=[row, pl.BlockSpec((TOK_TILE * SUBLANES, LANES), lambda i: (i, 0)), lane_row, lane_row],
        out_shape=[jax.ShapeDtypeStruct((n_rows, D_MODEL), F32),
                   jax.ShapeDtypeStruct((n_rows * SUBLANES, LANES), F32),
                   jax.ShapeDtypeStruct((n_rows, LANES), jnp.int32),
                   jax.ShapeDtypeStruct((n_rows, LANES), F32)],
        compiler_params=_params(("arbitrary",)), name="proj_out",
    )(*xs, *ws, h, mod, ln[0:1], ln[1:2], mod, mod, w_router, b_router)


def _split3(x):
    hi = x.astype(BF16)
    r = x - hi.astype(F32)
    mid = r.astype(BF16)
    lo = (r - mid.astype(F32)).astype(BF16)
    return hi, mid, lo


MOE_UNROLL = 8


MOE_TRASH = TOK_TILE
TILE_ROWS = MOE_TILE * SUBLANES
MOE_SLOT_BITS = 15
MOE_SLOT_MASK = (1 << MOE_SLOT_BITS) - 1


def _moe_kernel(goff_ref, cnt_ref, tok_hbm, gw_hbm, f_ref, wgu_ref, bgu_ref, wd_ref, bd_ref,
                o_ref, xin_ref, y_ref, tok_s, gw_s, sem, *, n_slots, chunk):
    c = pl.program_id(0)
    e = pl.program_id(1)
    n_assign = chunk * TOP_K

    @pl.when(e == 0)
    def _():
        cp_tok = pltpu.make_async_copy(tok_hbm.at[pl.ds(pl.multiple_of(c * n_slots, 1024), n_slots)],
                                       tok_s.at[pl.ds(0, n_slots)], sem.at[0])
        cp_gw = pltpu.make_async_copy(gw_hbm.at[pl.ds(pl.multiple_of(c * n_assign, 1024), n_assign)],
                                      gw_s.at[pl.ds(0, n_assign)], sem.at[1])
        cp_tok.start()
        cp_gw.start()
        o_ref[...] = jnp.zeros(o_ref.shape, F32)
        cp_tok.wait()
        cp_gw.wait()

    @pl.when((c == 0) & (e == 0))
    def _():
        xin_ref[...] = jnp.zeros(xin_ref.shape, F32)
        y_ref[...] = jnp.zeros(y_ref.shape, F32)
        for j in range(MOE_TILE):
            tok_s[n_slots + j] = 0
        for j in range(MOE_TILE * TOP_K):
            gw_s[n_assign + j] = 0.0

    g = c * N_EXPERTS + e
    group0 = goff_ref[g]
    n_rows = cnt_ref[g]
    n_tiles = (n_rows + MOE_TILE - 1) // MOE_TILE

    def rows8(i):
        return pl.ds(pl.multiple_of(i * SUBLANES, SUBLANES), SUBLANES)

    def gather(parity, off):
        for j in range(MOE_TILE):
            t = (tok_s[off + j] & MOE_SLOT_MASK) >> 2
            t = jnp.minimum(t, chunk - 1)
            xin_ref[rows8(parity * MOE_TILE + j), :] = f_ref[rows8(t), :]

    def scatter(parity, off, live):
        for j0 in range(0, MOE_TILE, MOE_UNROLL):
            ts, gs = [], []
            for j in range(j0, j0 + MOE_UNROLL):
                n = tok_s[off + j] & MOE_SLOT_MASK
                t, gwt = n >> 2, gw_s[n]
                if live is not None:
                    t = jnp.where(live, t, chunk + j)
                    gwt = jnp.where(live, gwt, 0.0)
                ts.append(rows8(t))
                gs.append(gwt)
            olds = [o_ref[d, :] for d in ts]
            for u, (d, gwt, old) in enumerate(zip(ts, gs, olds)):
                o_ref[d, :] = old + gwt * y_ref[rows8(parity * MOE_TILE + j0 + u), :]

    @pl.when(n_rows > 0)
    def _():
        gather(0, group0)

        def tile(r, carry):
            parity = r & 1
            off = group0 + r * MOE_TILE
            gather(1 - parity, off + MOE_TILE)
            base = parity * TILE_ROWS
            x = jnp.concatenate([xin_ref[pl.ds(base + s, MOE_TILE, stride=SUBLANES), :] for s in range(SUBLANES)],
                                axis=1).astype(BF16)
            gu = _dot(x, wgu_ref[0, 0]) + bgu_ref[0, 0]
            gate = jnp.minimum(gu[:, :D_EXPERT], SWIGLU_LIMIT)
            up = jnp.clip(gu[:, D_EXPERT:], -SWIGLU_LIMIT, SWIGLU_LIMIT)
            hid = gate * _sigmoid(SWIGLU_ALPHA * gate) * (up + 1.0)
            y = _dot(hid.astype(BF16), wd_ref[0, 0]) + bd_ref[0, 0]
            scatter(1 - parity, jnp.maximum(off - MOE_TILE, 0), r > 0)
            for s in range(SUBLANES):
                y_ref[pl.ds(base + s, MOE_TILE, stride=SUBLANES), :] = y[:, s * LANES:(s + 1) * LANES]
            return carry

        lax.fori_loop(0, n_tiles, tile, 0)
        last = n_tiles - 1
        scatter(last & 1, group0 + last * MOE_TILE, None)


def _moe_chunk(n_rows):
    c = MOE_CHUNK_MAX
    while n_rows % c:
        c -= TOK_TILE
    return c


def _moe_block(i, chunk):
    per = chunk // TOK_TILE
    return (i // per) * (per + MOE_TRASH // TOK_TILE) + i % per


def _moe_metadata(idx, gates, chunk):
    n_rows = idx.shape[0]
    n_chunks = n_rows // chunk
    n_assign = chunk * TOP_K
    n_pad = N_EXPERTS * MOE_TILE
    assert (chunk + MOE_TILE) * TOP_K <= 1 << MOE_SLOT_BITS
    e = idx[:, :TOP_K].reshape(n_chunks, n_assign)
    real = ((2 * e) << MOE_SLOT_BITS) | jnp.arange(n_assign, dtype=jnp.int32)
    pad_e = jnp.arange(n_pad, dtype=jnp.int32) // MOE_TILE
    pad_n = (chunk + jnp.arange(n_pad, dtype=jnp.int32) % MOE_TILE) * TOP_K
    pads = ((2 * pad_e + 1) << MOE_SLOT_BITS) | pad_n
    slots = jnp.sort(jnp.concatenate([real, jnp.broadcast_to(pads, (n_chunks, n_pad))], axis=1), axis=1)
    cnt = jnp.sum((e[:, :, None] == jnp.arange(N_EXPERTS, dtype=jnp.int32)).astype(jnp.int32), axis=1)
    goff = jnp.cumsum(cnt + MOE_TILE, axis=1) - (cnt + MOE_TILE)
    return slots.reshape(-1), gates[:, :TOP_K].reshape(-1), goff.reshape(-1), cnt.reshape(-1), n_assign + n_pad


def _moe(f_moe, idx, gates, wgu, bgu, wd, bd, layer):
    n_rows = idx.shape[0]
    chunk = _moe_chunk(n_rows)
    tok, gw, goff, cnt, n_slots = _moe_metadata(idx, gates, chunk)
    assert n_slots % 1024 == 0 and (chunk * TOP_K) % 1024 == 0 and MOE_TRASH >= MOE_TILE
    out_rows = (chunk + MOE_TRASH) * SUBLANES
    hbm = pl.BlockSpec(memory_space=pl.ANY)
    grid_spec = pltpu.PrefetchScalarGridSpec(
        num_scalar_prefetch=2,
        grid=(n_rows // chunk, N_EXPERTS),
        in_specs=[hbm, hbm,
                  pl.BlockSpec((chunk * SUBLANES, LANES), lambda c, e, *_: (c, 0), pipeline_mode=pl.Buffered(1)),
                  pl.BlockSpec((1, 1, D_MODEL, 2 * D_EXPERT), lambda c, e, *_: (layer, e, 0, 0)),
                  pl.BlockSpec((1, 1, 1, 2 * D_EXPERT), lambda c, e, *_: (layer, e, 0, 0)),
                  pl.BlockSpec((1, 1, D_EXPERT, D_MODEL), lambda c, e, *_: (layer, e, 0, 0)),
                  pl.BlockSpec((1, 1, 1, D_MODEL), lambda c, e, *_: (layer, e, 0, 0))],
        out_specs=pl.BlockSpec((out_rows, LANES), lambda c, e, *_: (c, 0), pipeline_mode=pl.Buffered(1)),
        scratch_shapes=[pltpu.VMEM((2 * TILE_ROWS, LANES), F32), pltpu.VMEM((2 * TILE_ROWS, LANES), F32),
                        pltpu.SMEM((n_slots + MOE_TILE,), jnp.int32),
                        pltpu.SMEM(((chunk + MOE_TILE) * TOP_K,), F32),
                        pltpu.SemaphoreType.DMA((2,))],
    )
    return pl.pallas_call(
        functools.partial(_moe_kernel, n_slots=n_slots, chunk=chunk),
        grid_spec=grid_spec,
        out_shape=jax.ShapeDtypeStruct((n_rows // chunk * out_rows, LANES), F32),
        compiler_params=_params(("arbitrary", "arbitrary")), name="moe",
    )(goff, cnt, tok, gw, f_moe, wgu, bgu, wd, bd), chunk


ATT_ROWS = 32
ATT_ROWS_UNROLL = 8


def _na_kernel(q_ref, k_ref, v_ref, kc_ref, vc_ref, bias_ref, o_ref, *, n_blocks, n_keys):
    blk = pl.program_id(2)
    key_row0 = jnp.clip(NA_QROWS * blk - NA_WIN_H // 2, 0, n_blocks * NA_QROWS - NA_KROWS)
    start = pl.multiple_of(key_row0 * GRID_W, GRID_W)
    q = q_ref[...]
    kl = k_ref[pl.ds(start, n_keys), :]
    vl = v_ref[pl.ds(start, n_keys), :]
    kc = kc_ref[...]
    vc = vc_ref[...]
    lane = lax.broadcasted_iota(jnp.int32, (1, LANES), 1)
    scale = NA_HEAD_DIM ** -0.5
    outs = []
    for a in range(2):
        sel = (lane < NA_HEAD_DIM) if a == 0 else (lane >= NA_HEAD_DIM)
        qa = jnp.where(sel, q, jnp.zeros_like(q)) * scale
        s_lat = _dot_nt(qa, kl) + bias_ref[0, a]
        s_ctx = _dot_nt(qa, kc)
        m = jnp.maximum(jnp.max(s_lat, axis=1, keepdims=True), jnp.max(s_ctx, axis=1, keepdims=True))
        e_lat = jnp.exp(s_lat - m)
        e_ctx = jnp.exp(s_ctx - m)
        den = jnp.sum(e_lat, axis=1, keepdims=True) + jnp.sum(e_ctx, axis=1, keepdims=True)
        outs.append((_dot(e_lat.astype(BF16), vl) + _dot(e_ctx.astype(BF16), vc)) / den)
    o_ref[...] = jnp.where(lane < NA_HEAD_DIM, outs[0], outs[1]).astype(BF16)


def _na_bias_table(rpb, rows):
    H = rpb.shape[0]
    qc = np.arange(GRID_W)
    col_start = np.clip(qc - NA_WIN_W // 2, 0, GRID_W - NA_WIN_W)
    col_in = (qc[None, :] >= col_start[:, None]) & (qc[None, :] < col_start[:, None] + NA_WIN_W)
    dx = np.clip(qc[None, :] - qc[:, None], 1 - NA_WIN_W, NA_WIN_W - 1) + NA_WIN_W - 1
    n_blocks = rows // NA_QROWS
    pick_y, masks = [], []
    for blk in (0, 1, n_blocks - 1):
        r = NA_QROWS * blk + np.arange(NA_QROWS)
        key_row0 = int(np.clip(NA_QROWS * blk - NA_WIN_H // 2, 0, rows - NA_KROWS))
        kr = key_row0 + np.arange(NA_KROWS)
        win0 = np.clip(r - NA_WIN_H // 2, 0, rows - NA_WIN_H)
        row_in = (kr[None, :] >= win0[:, None]) & (kr[None, :] < win0[:, None] + NA_WIN_H)
        dy = np.clip(kr[None, :] - r[:, None] + NA_WIN_H - 1, 0, 2 * NA_WIN_H - 2)
        pick_y.append(np.eye(2 * NA_WIN_H - 1, dtype=np.float32)[dy.reshape(-1)])
        masks.append(row_in[:, None, :, None] & col_in[None, :, None, :])
    pick_x = np.eye(2 * NA_WIN_W - 1, dtype=np.float32)[dx.reshape(-1)]
    bias = jnp.einsum('hyx,cay,bx->chab', rpb.astype(F32), jnp.asarray(np.stack(pick_y)), jnp.asarray(pick_x),
                      precision=lax.Precision.HIGHEST)
    bias = bias.reshape(3, H, NA_QROWS, NA_KROWS, GRID_W, GRID_W).transpose(0, 1, 2, 4, 3, 5)
    tabs = jnp.where(jnp.asarray(np.stack(masks))[:, None], bias, NEG_INF)
    tabs = tabs.reshape(3, H, NA_QROWS * GRID_W, NA_KROWS * GRID_W)
    return jnp.concatenate([tabs, jnp.full_like(tabs[:1], NEG_INF)], axis=0)


def _neighbourhood_attention(att, bias_tab, n_batch, n_lat, n_ctx):
    T = att.shape[0]
    qrows = NA_QROWS * GRID_W
    n_blocks = n_lat // qrows
    n_keys = NA_KROWS * GRID_W
    pairs = NA_WIDTH // LANES
    assert n_ctx == qrows
    lat_blocks = n_batch * n_blocks

    def q_map(b, hp, blk):
        return (jnp.where(blk < n_blocks, b * n_blocks + blk, lat_blocks + b), hp)

    def cls_map(b, hp, blk):
        cls = jnp.where(blk == 0, 0, jnp.where(blk == n_blocks - 1, 2, jnp.where(blk == n_blocks, 3, 1)))
        return (cls, hp, 0, 0)

    ctx_row = n_batch * n_lat // n_ctx
    return pl.pallas_call(
        functools.partial(_na_kernel, n_blocks=n_blocks, n_keys=n_keys),
        grid=(n_batch, pairs, n_blocks + 1),
        in_specs=[pl.BlockSpec((qrows, LANES), q_map),
                  pl.BlockSpec((n_lat, LANES), lambda b, hp, blk: (b, pairs + hp)),
                  pl.BlockSpec((n_lat, LANES), lambda b, hp, blk: (b, 2 * pairs + hp)),
                  pl.BlockSpec((n_ctx, LANES), lambda b, hp, blk: (ctx_row + b, pairs + hp)),
                  pl.BlockSpec((n_ctx, LANES), lambda b, hp, blk: (ctx_row + b, 2 * pairs + hp)),
                  pl.BlockSpec((1, 2, qrows, n_keys), cls_map)],
        out_specs=pl.BlockSpec((qrows, LANES), q_map),
        out_shape=jax.ShapeDtypeStruct((T, NA_WIDTH), BF16),
        compiler_params=_params(("arbitrary", "arbitrary", "arbitrary")), name="na_attention",
    )(att, att, att, att, att, bias_tab)


SWA_BLOCK = 128
SWA_BAND = 3 * SWA_BLOCK
def _swa_mask_table(n_lat):
    r = np.arange(SWA_BLOCK)[:, None]
    j = np.arange(SWA_BAND)[None, :]
    tabs = []
    n_blocks = n_lat // SWA_BLOCK
    for nb in (0, 1, n_blocks - 1):
        start = int(np.clip((nb - 1) * SWA_BLOCK, 0, n_lat - SWA_BAND))
        delta = nb * SWA_BLOCK + r - (start + j)
        tabs.append(np.where(np.abs(delta) <= SWA_WINDOW, 0.0, NEG_INF))
    tabs.append(np.full((SWA_BLOCK, SWA_BAND), NEG_INF))
    return jnp.asarray(np.stack(tabs), F32)


def _swa_kernel(sink_ref, q_ref, k_ref, v_ref, kc_ref, vc_ref, mask_ref, o_ref, s_ref, p_ref, den_ref, *, n_lat, group):
    kvh = pl.program_id(1)
    nb = pl.program_id(2)
    start = pl.multiple_of(jnp.clip((nb - 1) * SWA_BLOCK, 0, n_lat - SWA_BAND), SWA_BLOCK)
    kl = k_ref[pl.ds(start, SWA_BAND), :]
    vl = v_ref[pl.ds(start, SWA_BAND), :]
    q = q_ref[...]
    lane = lax.broadcasted_iota(jnp.int32, (1, LANES), 1)
    first_head = (lane & (SWA_HEAD_DIM - 1)) < SWA_HEAD_DIM // 2
    scale = SWA_HEAD_DIM ** -0.5
    parts = []
    for g in range(group):
        qp = q[:, (g // 2) * LANES:(g // 2 + 1) * LANES]
        sel = first_head if g % 2 == 0 else jnp.logical_not(first_head)
        parts.append(jnp.where(sel, qp, jnp.zeros_like(qp)) * scale)
    qs = jnp.concatenate(parts, axis=0)
    s_ref[:, :SWA_BAND] = _dot_nt(qs, kl)
    s_ref[:, SWA_BAND:] = _dot_nt(qs, kc_ref[...])

    def softmax_rows(i, carry):
        for u in range(ATT_ROWS_UNROLL):
            r0 = pl.multiple_of((i * ATT_ROWS_UNROLL + u) * ATT_ROWS, ATT_ROWS)
            rows = pl.ds(r0, ATT_ROWS)
            mrows = pl.ds(pl.multiple_of(r0 & (SWA_BLOCK - 1), ATT_ROWS), ATT_ROWS)
            s_lat = s_ref[rows, :SWA_BAND] + mask_ref[0, mrows, :]
            s_ctx = s_ref[rows, SWA_BAND:]
            blocks = ([s_lat[:, c * LANES:(c + 1) * LANES] for c in range(SWA_BAND // LANES)]
                      + [s_ctx[:, c * LANES:(c + 1) * LANES] for c in range(s_ctx.shape[1] // LANES)])
            sink = jnp.full((ATT_ROWS, 1), sink_ref[kvh * group + r0 // SWA_BLOCK], F32)
            m = jnp.maximum(jnp.max(functools.reduce(jnp.maximum, blocks), axis=1, keepdims=True), sink)
            es = [jnp.exp(b - m) for b in blocks]
            den = jnp.sum(functools.reduce(jnp.add, es), axis=1, keepdims=True) + jnp.exp(sink - m)
            for c, e in enumerate(es):
                p_ref[rows, c * LANES:(c + 1) * LANES] = e.astype(BF16)
            den_ref[rows, :] = jnp.broadcast_to(den, (ATT_ROWS, LANES))
        return carry

    lax.fori_loop(0, group * SWA_BLOCK // (ATT_ROWS * ATT_ROWS_UNROLL), softmax_rows, 0)
    o = (_dot(p_ref[:, :SWA_BAND], vl) + _dot(p_ref[:, SWA_BAND:], vc_ref[...])) / den_ref[...]
    for p in range(group // 2):
        o_a = o[(2 * p) * SWA_BLOCK:(2 * p + 1) * SWA_BLOCK]
        o_b = o[(2 * p + 1) * SWA_BLOCK:(2 * p + 2) * SWA_BLOCK]
        o_ref[:, p * LANES:(p + 1) * LANES] = jnp.where(lane < SWA_HEAD_DIM, o_a, o_b).astype(BF16)


def _window_attention(att, sink, n_batch, n_lat, n_ctx, with_ctx):
    T = att.shape[0]
    group = SWA_Q_HEADS // SWA_KV_HEADS
    qw = group * SWA_HEAD_DIM
    n_lat_blocks = n_lat // SWA_BLOCK
    n_ctx_blocks = n_ctx // SWA_BLOCK if with_ctx else 0
    k_col = SWA_Q_HEADS * SWA_HEAD_DIM // LANES
    v_col = k_col + SWA_KV_HEADS
    ctx_row = n_batch * n_lat // n_ctx

    def q_map(b, kvh, nb):
        lat = b * n_lat_blocks + nb
        ctx = n_batch * n_lat_blocks + b * (n_ctx // SWA_BLOCK) + (nb - n_lat_blocks)
        return (jnp.where(nb < n_lat_blocks, lat, ctx), kvh)

    def cls_map(b, kvh, nb):
        cls = jnp.where(nb == 0, 0, jnp.where(nb == n_lat_blocks - 1, 2, jnp.where(nb >= n_lat_blocks, 3, 1)))
        return (cls, 0, 0)

    grid_spec = pltpu.PrefetchScalarGridSpec(
        num_scalar_prefetch=0,
        grid=(n_batch, SWA_KV_HEADS, n_lat_blocks + n_ctx_blocks),
        in_specs=[pl.BlockSpec(memory_space=pltpu.SMEM),
                  pl.BlockSpec((SWA_BLOCK, qw), q_map),
                  pl.BlockSpec((n_lat, LANES), lambda b, kvh, nb: (b, k_col + kvh)),
                  pl.BlockSpec((n_lat, LANES), lambda b, kvh, nb: (b, v_col + kvh)),
                  pl.BlockSpec((n_ctx, LANES), lambda b, kvh, nb: (ctx_row + b, k_col + kvh)),
                  pl.BlockSpec((n_ctx, LANES), lambda b, kvh, nb: (ctx_row + b, v_col + kvh)),
                  pl.BlockSpec((1, SWA_BLOCK, SWA_BAND), cls_map)],
        out_specs=pl.BlockSpec((SWA_BLOCK, qw), q_map),
        scratch_shapes=[pltpu.VMEM((group * SWA_BLOCK, SWA_BAND + n_ctx), F32),
                        pltpu.VMEM((group * SWA_BLOCK, SWA_BAND + n_ctx), BF16),
                        pltpu.VMEM((group * SWA_BLOCK, LANES), F32)],
    )
    return pl.pallas_call(
        functools.partial(_swa_kernel, n_lat=n_lat, group=group),
        grid_spec=grid_spec,
        out_shape=jax.ShapeDtypeStruct((T if with_ctx else n_batch * n_lat, SWA_Q_HEADS * SWA_HEAD_DIM), BF16),
        compiler_params=_params(("arbitrary", "arbitrary", "arbitrary")), name="window_attention",
    )(sink.astype(F32), att, att, att, att, att, _swa_mask_table(n_lat))


DN_PAIR = 2 * DN_CHUNK
DN_GROUP = 4
DN_NQ = 2 * (DN_HEAD_DIM + DN_CHUNK)


def _bdot(a, b):
    return jnp.einsum('gik,gkj->gij', a, b, preferred_element_type=F32)


def _bdot_nt(a, b):
    return jnp.einsum('gik,gjk->gij', a, b, preferred_element_type=F32)


def _split2(x):
    hi = x.astype(BF16)
    return hi, (x - hi.astype(F32)).astype(BF16)


def _softplus(x):
    return jnp.maximum(x, 0.0) + jnp.log(1.0 + jnp.exp(-jnp.abs(x)))


def _pack_dot(a0, b0, a1, b1):
    z = jnp.zeros_like(b0)
    rhs = jnp.concatenate([jnp.concatenate([b0, z], axis=2), jnp.concatenate([z, b1], axis=2)], axis=1)
    r = _bdot(jnp.concatenate([a0, a1], axis=2), rhs)
    return r[:, :, :DN_PAIR], r[:, :, DN_PAIR:]


def _pack_dot3(a0, b0, a1, b1):
    (a0h, a0l), (b0h, b0l), (a1h, a1l), (b1h, b1l) = _split2(a0), _split2(b0), _split2(a1), _split2(b1)
    hh = _pack_dot(a0h, b0h, a1h, b1h)
    hl = _pack_dot(a0h, b0l, a1h, b1l)
    lh = _pack_dot(a0l, b0h, a1l, b1h)
    return hh[0] + (hl[0] + lh[0]), hh[1] + (hl[1] + lh[1])


def _dn_prepare(q, k, v, gcs, betas):
    G = q.shape[0]
    ii = lax.broadcasted_iota(jnp.int32, (DN_PAIR, DN_PAIR), 0)
    jj = lax.broadcasted_iota(jnp.int32, (DN_PAIR, DN_PAIR), 1)
    same = (ii >> 6) == (jj >> 6)
    incl = (same & (ii >= jj), same & (ii <= jj))
    strict = (same & (ii > jj), same & (ii < jj))
    eye = (ii == jj).astype(F32)
    eye16 = jnp.broadcast_to(eye.astype(BF16), (G, DN_PAIR, DN_PAIR))
    lane0 = jnp.broadcast_to((jj == 0).astype(BF16), (G, DN_PAIR, DN_PAIR))
    g_rows = sum(_bdot_nt(lane0, jnp.concatenate([pf, pb], axis=1)) for pf, pb in zip(_split3(gcs[0]), _split3(gcs[1])))
    g_row = (g_rows[:, :, :DN_PAIR], g_rows[:, :, DN_PAIR:])
    decay = [jnp.where(incl[d], jnp.exp(jnp.where(incl[d], gcs[d] - g_row[d], 0.0)), 0.0) for d in range(2)]
    k16 = k.astype(BF16)
    k_t = _bdot_nt(eye16, k16).astype(BF16)
    kb = [k * betas[d] for d in range(2)]
    vb = [v * betas[d] for d in range(2)]
    kk = _pack_dot(kb[0].astype(BF16), k_t, kb[1].astype(BF16), k_t)
    m = [jnp.where(strict[d], kk[d] * decay[d], 0.0) for d in range(2)]
    inv = [eye - m[0], eye - m[1]]
    p = m
    for _ in range(5):
        p = _pack_dot3(p[0], p[0], p[1], p[1])
        upd = _pack_dot3(inv[0], p[0], inv[1], p[1])
        inv = [inv[0] + upd[0], inv[1] + upd[1]]
    qk = _bdot(q.astype(BF16), k_t)
    first_chunk = lax.broadcasted_iota(jnp.int32, (DN_PAIR, 1), 0) < DN_CHUNK
    uws, a_intras, qgs, kgs, egls = [], [], [], [], []
    for d in range(2):
        gc = gcs[d]
        eg = jnp.exp(gc)
        uws.append(_bdot(inv[d].astype(BF16), jnp.concatenate([kb[d] * eg, vb[d]], axis=2).astype(BF16)).astype(BF16))
        a_intras.append(jnp.where(incl[d], qk * decay[d], 0.0).astype(BF16))
        lo, hi = (DN_CHUNK - 1, DN_PAIR - 1) if d == 0 else (0, DN_CHUNK)
        g_last = jnp.where(first_chunk, gc[:, lo:lo + 1, :], gc[:, hi:hi + 1, :])
        kgs.append((k * jnp.exp(g_last - gc)).astype(BF16))
        qgs.append(q * eg)
        egls.append(jnp.exp(g_last))
    kg_ts = _bdot_nt(eye16, jnp.concatenate(kgs, axis=1)).astype(BF16)
    zero = jnp.zeros((G, DN_PAIR, 2 * LANES), BF16)
    res = []
    for d in range(2):
        wu = uws[d]
        au = _bdot(a_intras[d], wu)
        q_eff = (qgs[d] - au[:, :, :LANES]).astype(BF16)
        wu_halves = jnp.concatenate([jnp.where(first_chunk, wu, zero), jnp.where(first_chunk, zero, wu)], axis=2)
        nb = _bdot(kg_ts[:, :, d * DN_PAIR:(d + 1) * DN_PAIR], wu_halves)
        nq = jnp.concatenate([(-nb[:, :, :LANES]).astype(BF16), q_eff[:, :DN_CHUNK],
                              (-nb[:, :, 2 * LANES:3 * LANES]).astype(BF16), q_eff[:, DN_CHUNK:]], axis=1)
        bb = jnp.concatenate([nb[:, :, LANES:2 * LANES], nb[:, :, 3 * LANES:]], axis=1)
        res.append((nq, bb, au[:, :, LANES:], egls[d]))
    return res


def _dn_sequence(q_ref, k_ref, v_ref, z_ref, ab_ref, cw_refs, gpar_ref, nw_ref, o_ref, s0, head, T, sc):
    (qn, kn, vn, gcs, bts, nqs, bss, egls, outs) = sc
    row = lax.broadcasted_iota(jnp.int32, (T, 1), 0)

    def conv_silu(x, w_ref):
        acc = x * w_ref[DN_CONV // 2:DN_CONV // 2 + 1, :]
        for j in range(DN_CONV):
            d = j - DN_CONV // 2
            if d == 0:
                continue
            shifted = pltpu.roll(x, (-d) % T, axis=0)
            ok = (row + d >= 0) & (row + d < T)
            acc = acc + jnp.where(ok, shifted, 0.0) * w_ref[j:j + 1, :]
        return acc * _sigmoid(acc)

    q = conv_silu(q_ref[...], cw_refs[0])
    k = conv_silu(k_ref[...], cw_refs[1])
    v = conv_silu(v_ref[...], cw_refs[2])
    q = q * lax.rsqrt(jnp.sum(q * q, axis=1, keepdims=True) + RMS_EPS) * DN_HEAD_DIM ** -0.5
    k = k * lax.rsqrt(jnp.sum(k * k, axis=1, keepdims=True) + RMS_EPS)
    qn[pl.ds(0, T), :] = q
    kn[pl.ds(0, T), :] = k
    vn[pl.ds(0, T), :] = v

    ab = ab_ref[...]
    g_all = -jnp.exp(gpar_ref[0:1, :]) * _softplus(ab + gpar_ref[1:2, :])
    b_all = _sigmoid(ab)
    lane = lax.broadcasted_iota(jnp.int32, (1, LANES), 1)
    src = jnp.where(lane < 2 * DN_HEADS, g_all, b_all)
    krow = lax.broadcasted_iota(jnp.int32, (3 * LANES, 4 * LANES), 0) & (LANES - 1)
    kcol = lax.broadcasted_iota(jnp.int32, (3 * LANES, 4 * LANES), 1) >> 7
    pick = (krow == kcol * DN_HEADS + head).astype(BF16)
    cols = _dot(jnp.concatenate(_split3(src), axis=1), pick)

    pos = row & (DN_CHUNK - 1)
    for d in range(2):
        g = cols[:, d * LANES:(d + 1) * LANES]
        for sh in (1, 2, 4, 8, 16, 32):
            if d == 0:
                g = g + jnp.where(pos >= sh, pltpu.roll(g, sh, axis=0), 0.0)
            else:
                g = g + jnp.where(pos < DN_CHUNK - sh, pltpu.roll(g, T - sh, axis=0), 0.0)
        gcs[d][pl.ds(0, T), :] = g
        bts[d][pl.ds(0, T), :] = cols[:, (2 + d) * LANES:(3 + d) * LANES]

    n_pairs = T // DN_PAIR
    G = min(DN_GROUP, n_pairs)
    R = G * DN_PAIR

    def prepare(gi, carry):
        rows = pl.ds(pl.multiple_of(gi * R, R), R)
        shape = (G, DN_PAIR, LANES)
        qq = qn[rows, :].reshape(shape)
        kk = kn[rows, :].reshape(shape)
        vv = vn[rows, :].reshape(shape)
        res = _dn_prepare(qq, kk, vv, [gcs[d][rows, :].reshape(shape) for d in range(2)],
                          [bts[d][rows, :].reshape(shape) for d in range(2)])
        for d in range(2):
            nq, bb, o_const, egl = res[d]
            nqs[d][pl.ds(pl.multiple_of(gi * G * DN_NQ, G * DN_NQ), G * DN_NQ), :] = nq.reshape(G * DN_NQ, LANES)
            bss[d][pl.ds(pl.multiple_of(gi * 2 * R, 2 * R), 2 * R), :] = bb.reshape(2 * R, LANES)
            outs[d][rows, :] = o_const.reshape(R, LANES)
            egls[d][rows, :] = egl.reshape(R, LANES)
        return carry

    lax.fori_loop(0, n_pairs // G, prepare, 0)

    def chunk_step(S, pair, half, d):
        off = pl.multiple_of(pair * DN_PAIR + half * DN_CHUNK, DN_CHUNK)
        rows = pl.ds(off, DN_CHUNK)
        lhs = nqs[d][pl.ds(pl.multiple_of(pair * DN_NQ + half * (DN_NQ // 2), DN_NQ // 2), DN_NQ // 2), :]
        r = _dot(lhs, S.astype(BF16))
        outs[d][rows, :] = outs[d][rows, :] + r[DN_HEAD_DIM:]
        decay = jnp.broadcast_to(egls[d][pl.ds(off, 1), :], (DN_HEAD_DIM, LANES))
        add = bss[d][pl.ds(pl.multiple_of(pair * 2 * DN_HEAD_DIM + half * DN_HEAD_DIM, DN_HEAD_DIM), DN_HEAD_DIM), :]
        return S * decay + (r[:DN_HEAD_DIM] + add)

    def scan(p, carry):
        s_f, s_b = carry
        pb = n_pairs - 1 - p
        s_f = chunk_step(s_f, p, 0, 0)
        s_b = chunk_step(s_b, pb, 1, 1)
        s_f = chunk_step(s_f, p, 1, 0)
        s_b = chunk_step(s_b, pb, 0, 1)
        return s_f, s_b

    s_f, s_b = lax.fori_loop(0, n_pairs, scan, s0)

    o = outs[0][pl.ds(0, T), :] + outs[1][pl.ds(0, T), :]
    o = o * lax.rsqrt(jnp.mean(o * o, axis=1, keepdims=True) + RMS_EPS) * nw_ref[...]
    z = z_ref[...]
    o_ref[...] = (o * (z * _sigmoid(z))).astype(BF16)
    return s_f, s_b


def _dn_kernel(ql, kl, vl, zl, abl, qc, kc, vc, zc, abc, cwq, cwk, cwv, gpar, nw, ol, oc, *scratch, n_lat, n_ctx):
    head = pl.program_id(1)
    it = iter(scratch)
    sc = [next(it) if i < 3 else (next(it), next(it)) for i in range(9)]
    zero = jnp.zeros((DN_HEAD_DIM, DN_HEAD_DIM), F32)
    s_ctx = _dn_sequence(qc, kc, vc, zc, abc, (cwq, cwk, cwv), gpar, nw, oc, (zero, zero), head, n_ctx, sc)
    _dn_sequence(ql, kl, vl, zl, abl, (cwq, cwk, cwv), gpar, nw, ol, s_ctx, head, n_lat, sc)


def _delta_net(dn, conv_w, a_log, dt_bias, norm_w, n_batch, n_lat, n_ctx):
    H = DN_HEADS
    ctx_row = n_batch * n_lat // n_ctx
    ab_col = 4 * H

    def lat(off):
        return pl.BlockSpec((n_lat, LANES), lambda b, h: (b, off + h))

    def ctx(off):
        return pl.BlockSpec((n_ctx, LANES), lambda b, h: (ctx_row + b, off + h))

    cw = jnp.zeros((SUBLANES, 3 * DN_WIDTH), F32).at[:DN_CONV].set(conv_w.astype(F32))
    gpar = jnp.zeros((SUBLANES, LANES), F32)
    gpar = gpar.at[0, :2 * H].set(a_log.reshape(-1).astype(F32)).at[1, :2 * H].set(dt_bias.reshape(-1).astype(F32))
    in_specs = [lat(0), lat(H), lat(2 * H), lat(3 * H), pl.BlockSpec((n_lat, LANES), lambda b, h: (b, ab_col)),
                ctx(0), ctx(H), ctx(2 * H), ctx(3 * H), pl.BlockSpec((n_ctx, LANES), lambda b, h: (ctx_row + b, ab_col)),
                pl.BlockSpec((SUBLANES, LANES), lambda b, h: (0, h)),
                pl.BlockSpec((SUBLANES, LANES), lambda b, h: (0, H + h)),
                pl.BlockSpec((SUBLANES, LANES), lambda b, h: (0, 2 * H + h)),
                pl.BlockSpec((SUBLANES, LANES), lambda b, h: (0, 0)),
                pl.BlockSpec((1, LANES), lambda b, h: (0, 0))]
    f32s = lambda rows=n_lat: pltpu.VMEM((rows, LANES), F32)
    n_pairs = n_lat // DN_PAIR
    scratch = [f32s(), f32s(), f32s()]
    scratch += [f32s() for _ in range(4)]
    scratch += [pltpu.VMEM((n_pairs * DN_NQ, LANES), BF16) for _ in range(2)]
    scratch += [f32s(2 * n_pairs * DN_HEAD_DIM) for _ in range(2)]
    scratch += [f32s() for _ in range(4)]
    return pl.pallas_call(
        functools.partial(_dn_kernel, n_lat=n_lat, n_ctx=n_ctx),
        grid=(n_batch, H),
        in_specs=in_specs,
        out_specs=[pl.BlockSpec((n_lat, LANES), lambda b, h: (b, h)),
                   pl.BlockSpec((n_ctx, LANES), lambda b, h: (b, h))],
        out_shape=[jax.ShapeDtypeStruct((n_batch * n_lat, DN_WIDTH), BF16),
                   jax.ShapeDtypeStruct((n_batch * n_ctx, DN_WIDTH), BF16)],
        scratch_shapes=scratch,
        compiler_params=_params(("arbitrary", "arbitrary")), name="delta_net",
    )(dn, dn, dn, dn, dn, dn, dn, dn, dn, dn, cw, cw, cw, gpar, norm_w.reshape(1, -1).astype(F32))


def _rope_tables(n_tokens):
    t = jnp.arange(n_tokens)
    n_freq = SWA_HEAD_DIM // 4
    inv = ROPE_THETA ** (-jnp.arange(n_freq, dtype=F32) / n_freq)
    row = (t // GRID_W).astype(F32)[:, None]
    col = (t % GRID_W).astype(F32)[:, None]
    ang = jnp.concatenate([row * inv, col * inv], -1)
    cos, sin = jnp.cos(ang), jnp.sin(ang)
    cos_t = jnp.concatenate([cos, cos, cos, cos], -1)
    sin_t = jnp.concatenate([-sin, -sin, sin, sin], -1)
    ident = jnp.ones((TOK_TILE, LANES), F32)
    return (jnp.concatenate([cos_t, ident], 0), jnp.concatenate([sin_t, jnp.zeros_like(ident)], 0))


def _swa_weight(w_in):
    D = w_in.shape[0]
    qw = SWA_Q_HEADS * SWA_HEAD_DIM
    kw = SWA_KV_HEADS * SWA_HEAD_DIM
    half = SWA_HEAD_DIM // 2
    wq = w_in[:, :qw].reshape(D, SWA_Q_HEADS // 2, 2, 2, half)
    wq = jnp.transpose(wq, (0, 1, 3, 2, 4)).reshape(D, qw)
    wk = w_in[:, qw:qw + kw].reshape(D, SWA_KV_HEADS, 2, 1, half)
    wk = jnp.broadcast_to(wk, (D, SWA_KV_HEADS, 2, 2, half)).reshape(D, 2 * kw)
    wv = w_in[:, qw + kw:].reshape(D, SWA_KV_HEADS, 1, SWA_HEAD_DIM)
    wv = jnp.broadcast_to(wv, (D, SWA_KV_HEADS, 2, SWA_HEAD_DIM)).reshape(D, 2 * kw)
    return jnp.concatenate([wq, wk, wv], axis=1).astype(BF16)


def kernel(x, c, ctx, c_ctx, w_mod, b_mod, ln_g, ln_b, w_in_ab, na_rpb, dn_conv, dn_a_log, dn_dt_bias, dn_norm_w,
           w_out_ab, w_in_c, swa_sink, w_out_c, w_router, b_router, w_gu, b_gu, w_down, b_down):
    B, N, D = x.shape
    L = ctx.shape[1]
    depth = w_mod.shape[0]
    alpha = (2 * depth) ** 0.25
    assert D == D_MODEL and N % TOK_TILE == 0 and L == TOK_TILE
    geom = (B * N // TOK_TILE, N // TOK_TILE, B)

    mod_rows = -(-(B + 1) // 16) * 16
    cc = jnp.zeros((mod_rows, D), F32).at[:B].set(c).at[B].set(c_ctx)
    mod_all = _modulation(cc, w_mod, b_mod)
    cos_t, sin_t = _rope_tables(N)
    n_att_ab = 3 * NA_WIDTH
    ab_main = n_att_ab + 4 * DN_WIDTH
    wr = jnp.zeros((depth, D, LANES), BF16).at[:, :, :N_EXPERTS].set(w_router.astype(BF16))
    br = jnp.full((depth, 1, LANES), NEG_INF, F32).at[:, 0, :N_EXPERTS].set(b_router.astype(F32))
    wgu16, wd16 = w_gu.astype(BF16), w_down.astype(BF16)
    bgu4, bd4 = b_gu[:, :, None, :].astype(F32), b_down[:, :, None, :].astype(F32)

    h = jnp.concatenate([x.reshape(B * N, D), ctx.reshape(B * L, D)], axis=0)
    moe_prev = mod_prev = ln_prev = None
    out = None
    for layer in range(depth):
        last = layer == depth - 1
        i = layer // 2
        mod = mod_all[layer].reshape(mod_rows, 1, -1)
        if layer % 2 == 0:
            w = w_in_ab[i]
            w = jnp.concatenate([w[:, :ab_main], jnp.pad(w[:, ab_main:], ((0, 0), (0, LANES - (w.shape[1] - ab_main))))], 1)
            h, att, dn = _proj_in(h, moe_prev, mod_prev, ln_prev, mod, w.astype(BF16), geom, n_att=n_att_ab, alpha=alpha)
            o_a = _neighbourhood_attention(att, _na_bias_table(na_rpb[i], N // GRID_W), B, N, L)
            o_bl, o_bc = _delta_net(dn, dn_conv[i], dn_a_log[i], dn_dt_bias[i], dn_norm_w[i], B, N, L)
            xs = [o_a, jnp.concatenate([o_bl, o_bc], axis=0)]
            wo = w_out_ab[i].astype(BF16)
            ws = [wo[:NA_WIDTH], wo[NA_WIDTH:]]
        else:
            w = _swa_weight(w_in_c[i])
            n_rope = (SWA_Q_HEADS + 2 * SWA_KV_HEADS) * SWA_HEAD_DIM // LANES
            h, att, _ = _proj_in(h, moe_prev, mod_prev, ln_prev, mod, w, geom, n_att=w.shape[1],
                                 rope=(cos_t, sin_t, n_rope), alpha=alpha)
            xs = [_window_attention(att, swa_sink[i], B, N, L, with_ctx=not last)]
            ws = [w_out_c[i].astype(BF16)]
        n_rows = B * N if last else B * (N + L)
        ln1 = jnp.stack([ln_g[layer, 0], ln_b[layer, 0]])
        ln2 = jnp.stack([ln_g[layer, 1], ln_b[layer, 1]])
        h1, f_moe, idx, gates = _proj_out(xs, ws, h, mod, ln1, wr[layer], br[layer], geom, n_rows, alpha=alpha)
        y_moe = _moe(f_moe, idx, gates, wgu16, bgu4, wd16, bd4, layer)
        if last:
            out = _final_combine(h1, y_moe, mod, ln2, geom, alpha=alpha).reshape(B, N, D)
        else:
            h, moe_prev, mod_prev, ln_prev = h1, y_moe, mod, ln2
    return out
```

```python
import functools

import jax
import jax.numpy as jnp
import numpy as np
from jax import lax
from jax.experimental import pallas as pl
from jax.experimental.pallas import tpu as pltpu

F32 = jnp.float32
BF16 = jnp.bfloat16

D_MODEL = 1024
GRID_W = 64
NA_HEADS = 8
NA_HEAD_DIM = 64
NA_WIN_H = 8
NA_WIN_W = 16
DN_HEADS = 4
DN_HEAD_DIM = 128
DN_CONV = 5
DN_CHUNK = 64
SWA_Q_HEADS = 16
SWA_KV_HEADS = 2
SWA_HEAD_DIM = 64
SWA_WINDOW = 128
ROPE_THETA = 10000.0
N_EXPERTS = 32
TOP_K = 4
D_EXPERT = 1024
SWIGLU_LIMIT = 7.0
SWIGLU_ALPHA = 1.702
LN_EPS = 1e-5
RMS_EPS = 1e-6
NEG_INF = -1e30

LANES = 128
SUBLANES = 8
TOK_TILE = 256
MOE_TILE = 128
MOE_CHUNK_MAX = 4608
VMEM_LIMIT = 60 * 1024 * 1024

NA_WIDTH = NA_HEADS * NA_HEAD_DIM
DN_WIDTH = DN_HEADS * DN_HEAD_DIM
NA_QROWS = 4
NA_KROWS = 12


def _params(sem):
    return pltpu.CompilerParams(dimension_semantics=sem, vmem_limit_bytes=VMEM_LIMIT)


def _sigmoid(x):
    return 1.0 / (1.0 + jnp.exp(-x))


def _dot(a, b):
    return jnp.dot(a, b, preferred_element_type=F32)


def _dot_nt(a, b):
    return lax.dot_general(a, b, (((1,), (1,)), ((), ())), preferred_element_type=F32)


def _moe_layout_load(ref, rows):
    return jnp.concatenate([ref[pl.ds(s, rows, stride=SUBLANES), :] for s in range(SUBLANES)], axis=1)


def _moe_layout_store(ref, val, rows):
    for s in range(SUBLANES):
        ref[pl.ds(s, rows, stride=SUBLANES), :] = val[:, s * LANES:(s + 1) * LANES]


def _layer_norm(y, g, b):
    mu = jnp.mean(y, axis=-1, keepdims=True)
    d = y - mu
    var = jnp.mean(d * d, axis=-1, keepdims=True)
    return d * lax.rsqrt(var + LN_EPS) * g + b


def _mod_kernel(s_ref, w_ref, b_ref, o_ref):
    s = s_ref[...]
    s = s * _sigmoid(s)
    o_ref[0] = _dot(s.astype(BF16), w_ref[0].astype(BF16)) + b_ref[0]


def _modulation(cc, w_mod, b_mod):
    depth = w_mod.shape[0]
    rows = cc.shape[0]
    nblk = w_mod.shape[2] // D_MODEL
    return pl.pallas_call(
        _mod_kernel,
        grid=(depth, nblk),
        in_specs=[pl.BlockSpec((rows, D_MODEL), lambda l, j: (0, 0)),
                  pl.BlockSpec((1, D_MODEL, D_MODEL), lambda l, j: (l, 0, j)),
                  pl.BlockSpec((1, 1, D_MODEL), lambda l, j: (l, 0, j))],
        out_specs=pl.BlockSpec((1, rows, D_MODEL), lambda l, j: (l, 0, j)),
        out_shape=jax.ShapeDtypeStruct((depth, rows, nblk * D_MODEL), F32),
        compiler_params=_params(("arbitrary", "arbitrary")),
        name="modulation",
    )(cc, w_mod, b_mod.reshape(depth, 1, -1))


def _pin_kernel(*refs, first, n_rope, n_att, alpha):
    refs = list(refs)
    h_ref = refs.pop(0)
    if not first:
        moe_ref, g2_ref, lg_ref, lb_ref = refs[:4]
        refs = refs[4:]
    sc_ref, sh_ref, w_ref = refs[:3]
    refs = refs[3:]
    if n_rope:
        cos_ref, sin_ref = refs[:2]
        refs = refs[2:]
    if not first:
        hout_ref = refs.pop(0)
    att_ref = refs.pop(0)
    dn_ref = refs.pop(0) if refs else None

    h = h_ref[...]
    if not first:
        m = _moe_layout_load(moe_ref, TOK_TILE)
        h = _layer_norm(alpha * h + g2_ref[0] * m, lg_ref[...], lb_ref[...])
        hout_ref[...] = h
    a = h * (1.0 + sc_ref[0]) + sh_ref[0]
    acc = _dot(a.astype(BF16), w_ref[...])
    if n_rope:
        cos = cos_ref[...]
        sin = sin_ref[...]
        for g in range(n_att // LANES):
            blk = acc[:, g * LANES:(g + 1) * LANES]
            if g < n_rope:
                blk = blk * cos + pltpu.roll(blk, LANES // 2, axis=1) * sin
            att_ref[:, g * LANES:(g + 1) * LANES] = blk.astype(BF16)
    else:
        att_ref[...] = acc[:, :n_att].astype(BF16)
    if dn_ref is not None:
        dn_ref[...] = acc[:, n_att:]


def _mod_spec(k, n_lat_tiles, lat_tiles_per_batch, n_batch):
    def imap(i):
        return (jnp.where(i < n_lat_tiles, i // lat_tiles_per_batch, n_batch), 0, k)
    return pl.BlockSpec((1, 1, D_MODEL), imap)


def _proj_in(h, moe_prev, mod_prev, ln_prev, mod, w, geom, *, n_att, rope=None, alpha=1.0):
    T = h.shape[0]
    n_tiles = T // TOK_TILE
    n_lat_tiles, per_batch, n_batch = geom
    first = moe_prev is None
    ncols = w.shape[1]
    row = pl.BlockSpec((TOK_TILE, D_MODEL), lambda i: (i, 0))
    vec = pl.BlockSpec((1, D_MODEL), lambda i: (0, 0))
    ms = functools.partial(_mod_spec, n_lat_tiles=n_lat_tiles, lat_tiles_per_batch=per_batch, n_batch=n_batch)
    args, specs = [h], [row]
    if not first:
        moe_arr, moe_chunk = moe_prev
        args += [moe_arr, mod_prev, ln_prev[0:1], ln_prev[1:2]]
        specs += [pl.BlockSpec((TOK_TILE * SUBLANES, LANES), lambda i: (_moe_block(i, moe_chunk), 0)), ms(5), vec, vec]
    args += [mod, mod, w]
    specs += [ms(1), ms(0), pl.BlockSpec((D_MODEL, ncols), lambda i: (0, 0))]
    n_rope = 0
    if rope is not None:
        cos_t, sin_t, n_rope = rope
        tab = pl.BlockSpec((TOK_TILE, LANES), lambda i: (jnp.where(i < n_lat_tiles, i % per_batch, per_batch), 0))
        args += [cos_t, sin_t]
        specs += [tab, tab]
    out_shapes, out_specs = [], []
    if not first:
        out_shapes.append(jax.ShapeDtypeStruct((T, D_MODEL), F32))
        out_specs.append(row)
    out_shapes.append(jax.ShapeDtypeStruct((T, n_att), BF16))
    out_specs.append(pl.BlockSpec((TOK_TILE, n_att), lambda i: (i, 0)))
    if ncols > n_att:
        out_shapes.append(jax.ShapeDtypeStruct((T, ncols - n_att), F32))
        out_specs.append(pl.BlockSpec((TOK_TILE, ncols - n_att), lambda i: (i, 0)))
    outs = pl.pallas_call(
        functools.partial(_pin_kernel, first=first, n_rope=n_rope, n_att=n_att, alpha=alpha),
        grid=(n_tiles,), in_specs=specs, out_specs=out_specs, out_shape=out_shapes,
        compiler_params=_params(("arbitrary",)), name="proj_in",
    )(*args)
    outs = list(outs)
    h_new = h if first else outs.pop(0)
    att = outs.pop(0)
    dn = outs.pop(0) if outs else None
    return h_new, att, dn


def _final_kernel(h_ref, moe_ref, g2_ref, lg_ref, lb_ref, o_ref, *, alpha):
    m = _moe_layout_load(moe_ref, TOK_TILE)
    o_ref[...] = _layer_norm(alpha * h_ref[...] + g2_ref[0] * m, lg_ref[...], lb_ref[...])


def _final_combine(h, moe, mod, ln, geom, *, alpha):
    T = h.shape[0]
    n_lat_tiles, per_batch, n_batch = geom
    moe, moe_chunk = moe
    row = pl.BlockSpec((TOK_TILE, D_MODEL), lambda i: (i, 0))
    vec = pl.BlockSpec((1, D_MODEL), lambda i: (0, 0))
    return pl.pallas_call(
        functools.partial(_final_kernel, alpha=alpha),
        grid=(T // TOK_TILE,),
        in_specs=[row, pl.BlockSpec((TOK_TILE * SUBLANES, LANES), lambda i: (_moe_block(i, moe_chunk), 0)),
                  _mod_spec(5, n_lat_tiles, per_batch, n_batch), vec, vec],
        out_specs=row, out_shape=jax.ShapeDtypeStruct((T, D_MODEL), F32),
        compiler_params=_params(("arbitrary",)), name="final_combine",
    )(h, moe, mod, ln[0:1], ln[1:2])


def _pout_kernel(*refs, n_in, alpha):
    xs = refs[:n_in]
    ws = refs[n_in:2 * n_in]
    (h_ref, g1_ref, lg_ref, lb_ref, sc_ref, sh_ref, wr_ref, br_ref,
     hout_ref, f_ref, idx_ref, gate_ref) = refs[2 * n_in:]
    m = _dot(xs[0][...], ws[0][...])
    for x_ref, w_ref in zip(xs[1:], ws[1:]):
        m = m + _dot(x_ref[...], w_ref[...])
    h1 = _layer_norm(alpha * h_ref[...] + g1_ref[0] * m, lg_ref[...], lb_ref[...])
    hout_ref[...] = h1
    f = h1 * (1.0 + sc_ref[0]) + sh_ref[0]
    _moe_layout_store(f_ref, f, TOK_TILE)
    logits = _dot(f.astype(BF16), wr_ref[...]) + br_ref[...]
    lane = lax.broadcasted_iota(jnp.int32, logits.shape, 1)
    work = logits
    vals, idxs = [], []
    for _ in range(TOP_K):
        v = jnp.max(work, axis=1, keepdims=True)
        ix = jnp.min(jnp.where(work == v, lane, LANES), axis=1, keepdims=True)
        vals.append(v)
        idxs.append(ix)
        work = jnp.where(lane == ix, -jnp.inf, work)
    es = [jnp.exp(v - vals[0]) for v in vals]
    den = es[0]
    for e in es[1:]:
        den = den + e
    idx_out = jnp.zeros(logits.shape, jnp.int32)
    gate_out = jnp.zeros(logits.shape, F32)
    for k in range(TOP_K):
        idx_out = jnp.where(lane == k, idxs[k], idx_out)
        gate_out = jnp.where(lane == k, es[k] / den, gate_out)
    idx_ref[...] = idx_out
    gate_ref[...] = gate_out


def _proj_out(xs, ws, h, mod, ln, w_router, b_router, geom, n_rows, *, alpha):
    n_lat_tiles, per_batch, n_batch = geom
    row = pl.BlockSpec((TOK_TILE, D_MODEL), lambda i: (i, 0))
    vec = pl.BlockSpec((1, D_MODEL), lambda i: (0, 0))
    lane_row = pl.BlockSpec((TOK_TILE, LANES), lambda i: (i, 0))
    ms = functools.partial(_mod_spec, n_lat_tiles=n_lat_tiles, lat_tiles_per_batch=per_batch, n_batch=n_batch)
    specs = [pl.BlockSpec((TOK_TILE, x.shape[1]), lambda i: (i, 0)) for x in xs]
    specs += [pl.BlockSpec(w.shape, lambda i: (0, 0)) for w in ws]
    specs += [row, ms(2), vec, vec, ms(4), ms(3),
              pl.BlockSpec((D_MODEL, LANES), lambda i: (0, 0)), pl.BlockSpec((1, LANES), lambda i: (0, 0))]
    return pl.pallas_call(
        functools.partial(_pout_kernel, n_in=len(xs), alpha=alpha),
        grid=(n_rows // TOK_TILE,), in_specs=specs,
        out_specs=[row, pl.BlockSpec((TOK_TILE * SUBLANES, LANES), lambda i: (i, 0)), lane_row, lane_row],
        out_shape=[jax.ShapeDtypeStruct((n_rows, D_MODEL), F32),
                   jax.ShapeDtypeStruct((n_rows * SUBLANES, LANES), F32),
                   jax.ShapeDtypeStruct((n_rows, LANES), jnp.int32),
                   jax.ShapeDtypeStruct((n_rows, LANES), F32)],
        compiler_params=_params(("arbitrary",)), name="proj_out",
    )(*xs, *ws, h, mod, ln[0:1], ln[1:2], mod, mod, w_router, b_router)


def _split3(x):
    hi = x.astype(BF16)
    r = x - hi.astype(F32)
    mid = r.astype(BF16)
    lo = (r - mid.astype(F32)).astype(BF16)
    return hi, mid, lo


MOE_UNROLL = 8


MOE_TRASH = TOK_TILE
TILE_ROWS = MOE_TILE * SUBLANES


def _moe_kernel(goff_ref, cnt_ref, tok_hbm, gw_hbm, f_ref, wgu_ref, bgu_ref, wd_ref, bd_ref,
                o_ref, xin_ref, y_ref, tok_s, gw_s, sem, *, n_slots, chunk):
    c = pl.program_id(0)
    e = pl.program_id(1)

    @pl.when(e == 0)
    def _():
        src = pl.ds(pl.multiple_of(c * n_slots, 1024), n_slots)
        cp_tok = pltpu.make_async_copy(tok_hbm.at[src], tok_s.at[pl.ds(0, n_slots)], sem.at[0])
        cp_gw = pltpu.make_async_copy(gw_hbm.at[src], gw_s.at[pl.ds(0, n_slots)], sem.at[1])
        cp_tok.start()
        cp_gw.start()
        o_ref[...] = jnp.zeros(o_ref.shape, F32)
        cp_tok.wait()
        cp_gw.wait()

    @pl.when((c == 0) & (e == 0))
    def _():
        xin_ref[...] = jnp.zeros(xin_ref.shape, F32)
        y_ref[...] = jnp.zeros(y_ref.shape, F32)
        for j in range(MOE_TILE):
            tok_s[n_slots + j] = 0
            gw_s[n_slots + j] = 0.0

    g = c * N_EXPERTS + e
    group0 = goff_ref[g]
    n_rows = cnt_ref[g]
    n_tiles = (n_rows + MOE_TILE - 1) // MOE_TILE

    def rows8(i):
        return pl.ds(pl.multiple_of(i * SUBLANES, SUBLANES), SUBLANES)

    def gather(parity, off):
        for j in range(MOE_TILE):
            t = jnp.minimum(tok_s[off + j], chunk - 1)
            xin_ref[rows8(parity * MOE_TILE + j), :] = f_ref[rows8(t), :]

    def scatter(parity, off, live):
        for j0 in range(0, MOE_TILE, MOE_UNROLL):
            ts, gs = [], []
            for j in range(j0, j0 + MOE_UNROLL):
                t, gwt = tok_s[off + j], gw_s[off + j]
                if live is not None:
                    t = jnp.where(live, t, chunk + j)
                    gwt = jnp.where(live, gwt, 0.0)
                ts.append(rows8(t))
                gs.append(gwt)
            olds = [o_ref[d, :] for d in ts]
            for u, (d, gwt, old) in enumerate(zip(ts, gs, olds)):
                o_ref[d, :] = old + gwt * y_ref[rows8(parity * MOE_TILE + j0 + u), :]

    @pl.when(n_rows > 0)
    def _():
        gather(0, group0)

        def tile(r, carry):
            parity = r & 1
            off = group0 + r * MOE_TILE
            gather(1 - parity, off + MOE_TILE)
            base = parity * TILE_ROWS
            x = jnp.concatenate([xin_ref[pl.ds(base + s, MOE_TILE, stride=SUBLANES), :] for s in range(SUBLANES)],
                                axis=1).astype(BF16)
            gu = _dot(x, wgu_ref[0, 0]) + bgu_ref[0, 0]
            gate = jnp.minimum(gu[:, :D_EXPERT], SWIGLU_LIMIT)
            up = jnp.clip(gu[:, D_EXPERT:], -SWIGLU_LIMIT, SWIGLU_LIMIT)
            hid = gate * _sigmoid(SWIGLU_ALPHA * gate) * (up + 1.0)
            y = _dot(hid.astype(BF16), wd_ref[0, 0]) + bd_ref[0, 0]
            scatter(1 - parity, jnp.maximum(off - MOE_TILE, 0), r > 0)
            for s in range(SUBLANES):
                y_ref[pl.ds(base + s, MOE_TILE, stride=SUBLANES), :] = y[:, s * LANES:(s + 1) * LANES]
            return carry

        lax.fori_loop(0, n_tiles, tile, 0)
        last = n_tiles - 1
        scatter(last & 1, group0 + last * MOE_TILE, None)


def _moe_chunk(n_rows):
    c = MOE_CHUNK_MAX
    while n_rows % c:
        c -= TOK_TILE
    return c


def _moe_block(i, chunk):
    per = chunk // TOK_TILE
    return (i // per) * (per + MOE_TRASH // TOK_TILE) + i % per


def _moe_metadata(idx, gates, chunk):
    n_rows = idx.shape[0]
    n_chunks = n_rows // chunk
    n_assign = chunk * TOP_K
    n_pad = N_EXPERTS * MOE_TILE
    e = idx[:, :TOP_K].reshape(n_chunks, n_assign)
    gv = gates[:, :TOP_K].reshape(n_chunks, n_assign)
    tokv = jnp.broadcast_to(jnp.arange(n_assign, dtype=jnp.int32) // TOP_K, (n_chunks, n_assign))
    pad_e = jnp.arange(n_pad, dtype=jnp.int32) // MOE_TILE
    pad_t = chunk + jnp.arange(n_pad, dtype=jnp.int32) % MOE_TILE
    keys = jnp.concatenate([2 * e, jnp.broadcast_to(2 * pad_e + 1, (n_chunks, n_pad))], axis=1)
    toks = jnp.concatenate([tokv, jnp.broadcast_to(pad_t, (n_chunks, n_pad))], axis=1)
    gws = jnp.concatenate([gv, jnp.zeros((n_chunks, n_pad), F32)], axis=1)
    _, tok, gw = lax.sort((keys, toks, gws), dimension=1, num_keys=1, is_stable=True)
    cnt = jnp.sum((e[:, :, None] == jnp.arange(N_EXPERTS, dtype=jnp.int32)).astype(jnp.int32), axis=1)
    goff = jnp.cumsum(cnt + MOE_TILE, axis=1) - (cnt + MOE_TILE)
    return tok.reshape(-1), gw.reshape(-1), goff.reshape(-1), cnt.reshape(-1), n_assign + n_pad


def _moe(f_moe, idx, gates, wgu, bgu, wd, bd, layer):
    n_rows = idx.shape[0]
    chunk = _moe_chunk(n_rows)
    tok, gw, goff, cnt, n_slots = _moe_metadata(idx, gates, chunk)
    assert n_slots % 1024 == 0 and MOE_TRASH >= MOE_TILE
    out_rows = (chunk + MOE_TRASH) * SUBLANES
    hbm = pl.BlockSpec(memory_space=pl.ANY)
    grid_spec = pltpu.PrefetchScalarGridSpec(
        num_scalar_prefetch=2,
        grid=(n_rows // chunk, N_EXPERTS),
        in_specs=[hbm, hbm,
                  pl.BlockSpec((chunk * SUBLANES, LANES), lambda c, e, *_: (c, 0), pipeline_mode=pl.Buffered(1)),
                  pl.BlockSpec((1, 1, D_MODEL, 2 * D_EXPERT), lambda c, e, *_: (layer, e, 0, 0)),
                  pl.BlockSpec((1, 1, 1, 2 * D_EXPERT), lambda c, e, *_: (layer, e, 0, 0)),
                  pl.BlockSpec((1, 1, D_EXPERT, D_MODEL), lambda c, e, *_: (layer, e, 0, 0)),
                  pl.BlockSpec((1, 1, 1, D_MODEL), lambda c, e, *_: (layer, e, 0, 0))],
        out_specs=pl.BlockSpec((out_rows, LANES), lambda c, e, *_: (c, 0), pipeline_mode=pl.Buffered(1)),
        scratch_shapes=[pltpu.VMEM((2 * TILE_ROWS, LANES), F32), pltpu.VMEM((2 * TILE_ROWS, LANES), F32),
                        pltpu.SMEM((n_slots + MOE_TILE,), jnp.int32), pltpu.SMEM((n_slots + MOE_TILE,), F32),
                        pltpu.SemaphoreType.DMA((2,))],
    )
    return pl.pallas_call(
        functools.partial(_moe_kernel, n_slots=n_slots, chunk=chunk),
        grid_spec=grid_spec,
        out_shape=jax.ShapeDtypeStruct((n_rows // chunk * out_rows, LANES), F32),
        compiler_params=_params(("arbitrary", "arbitrary")), name="moe",
    )(goff, cnt, tok, gw, f_moe, wgu, bgu, wd, bd), chunk


ATT_ROWS = 32
ATT_ROWS_UNROLL = 8


def _na_kernel(q_ref, k_ref, v_ref, kc_ref, vc_ref, bias_ref, o_ref, *, n_blocks, n_keys):
    blk = pl.program_id(2)
    key_row0 = jnp.clip(NA_QROWS * blk - NA_WIN_H // 2, 0, n_blocks * NA_QROWS - NA_KROWS)
    start = pl.multiple_of(key_row0 * GRID_W, GRID_W)
    q = q_ref[...]
    kl = k_ref[pl.ds(start, n_keys), :]
    vl = v_ref[pl.ds(start, n_keys), :]
    kc = kc_ref[...]
    vc = vc_ref[...]
    lane = lax.broadcasted_iota(jnp.int32, (1, LANES), 1)
    scale = NA_HEAD_DIM ** -0.5
    outs = []
    for a in range(2):
        sel = (lane < NA_HEAD_DIM) if a == 0 else (lane >= NA_HEAD_DIM)
        qa = jnp.where(sel, q, jnp.zeros_like(q)) * scale
        s_lat = _dot_nt(qa, kl) + bias_ref[0, a]
        s_ctx = _dot_nt(qa, kc)
        m = jnp.maximum(jnp.max(s_lat, axis=1, keepdims=True), jnp.max(s_ctx, axis=1, keepdims=True))
        e_lat = jnp.exp(s_lat - m)
        e_ctx = jnp.exp(s_ctx - m)
        den = jnp.sum(e_lat, axis=1, keepdims=True) + jnp.sum(e_ctx, axis=1, keepdims=True)
        outs.append((_dot(e_lat.astype(BF16), vl) + _dot(e_ctx.astype(BF16), vc)) / den)
    o_ref[...] = jnp.where(lane < NA_HEAD_DIM, outs[0], outs[1]).astype(BF16)


def _na_bias_table(rpb, rows):
    H = rpb.shape[0]
    qc = np.arange(GRID_W)
    col_start = np.clip(qc - NA_WIN_W // 2, 0, GRID_W - NA_WIN_W)
    col_in = (qc[None, :] >= col_start[:, None]) & (qc[None, :] < col_start[:, None] + NA_WIN_W)
    dx = np.clip(qc[None, :] - qc[:, None], 1 - NA_WIN_W, NA_WIN_W - 1) + NA_WIN_W - 1
    n_blocks = rows // NA_QROWS
    pick_y, masks = [], []
    for blk in (0, 1, n_blocks - 1):
        r = NA_QROWS * blk + np.arange(NA_QROWS)
        key_row0 = int(np.clip(NA_QROWS * blk - NA_WIN_H // 2, 0, rows - NA_KROWS))
        kr = key_row0 + np.arange(NA_KROWS)
        win0 = np.clip(r - NA_WIN_H // 2, 0, rows - NA_WIN_H)
        row_in = (kr[None, :] >= win0[:, None]) & (kr[None, :] < win0[:, None] + NA_WIN_H)
        dy = np.clip(kr[None, :] - r[:, None] + NA_WIN_H - 1, 0, 2 * NA_WIN_H - 2)
        pick_y.append(np.eye(2 * NA_WIN_H - 1, dtype=np.float32)[dy.reshape(-1)])
        masks.append(row_in[:, None, :, None] & col_in[None, :, None, :])
    pick_x = np.eye(2 * NA_WIN_W - 1, dtype=np.float32)[dx.reshape(-1)]
    bias = jnp.einsum('hyx,cay,bx->chab', rpb.astype(F32), jnp.asarray(np.stack(pick_y)), jnp.asarray(pick_x),
                      precision=lax.Precision.HIGHEST)
    bias = bias.reshape(3, H, NA_QROWS, NA_KROWS, GRID_W, GRID_W).transpose(0, 1, 2, 4, 3, 5)
    tabs = jnp.where(jnp.asarray(np.stack(masks))[:, None], bias, NEG_INF)
    tabs = tabs.reshape(3, H, NA_QROWS * GRID_W, NA_KROWS * GRID_W)
    return jnp.concatenate([tabs, jnp.full_like(tabs[:1], NEG_INF)], axis=0)


def _neighbourhood_attention(att, bias_tab, n_batch, n_lat, n_ctx):
    T = att.shape[0]
    qrows = NA_QROWS * GRID_W
    n_blocks = n_lat // qrows
    n_keys = NA_KROWS * GRID_W
    pairs = NA_WIDTH // LANES
    assert n_ctx == qrows
    lat_blocks = n_batch * n_blocks

    def q_map(b, hp, blk):
        return (jnp.where(blk < n_blocks, b * n_blocks + blk, lat_blocks + b), hp)

    def cls_map(b, hp, blk):
        cls = jnp.where(blk == 0, 0, jnp.where(blk == n_blocks - 1, 2, jnp.where(blk == n_blocks, 3, 1)))
        return (cls, hp, 0, 0)

    ctx_row = n_batch * n_lat // n_ctx
    return pl.pallas_call(
        functools.partial(_na_kernel, n_blocks=n_blocks, n_keys=n_keys),
        grid=(n_batch, pairs, n_blocks + 1),
        in_specs=[pl.BlockSpec((qrows, LANES), q_map),
                  pl.BlockSpec((n_lat, LANES), lambda b, hp, blk: (b, pairs + hp)),
                  pl.BlockSpec((n_lat, LANES), lambda b, hp, blk: (b, 2 * pairs + hp)),
                  pl.BlockSpec((n_ctx, LANES), lambda b, hp, blk: (ctx_row + b, pairs + hp)),
                  pl.BlockSpec((n_ctx, LANES), lambda b, hp, blk: (ctx_row + b, 2 * pairs + hp)),
                  pl.BlockSpec((1, 2, qrows, n_keys), cls_map)],
        out_specs=pl.BlockSpec((qrows, LANES), q_map),
        out_shape=jax.ShapeDtypeStruct((T, NA_WIDTH), BF16),
        compiler_params=_params(("arbitrary", "arbitrary", "arbitrary")), name="na_attention",
    )(att, att, att, att, att, bias_tab)


SWA_BLOCK = 128
SWA_BAND = 3 * SWA_BLOCK
def _swa_mask_table(n_lat):
    r = np.arange(SWA_BLOCK)[:, None]
    j = np.arange(SWA_BAND)[None, :]
    tabs = []
    n_blocks = n_lat // SWA_BLOCK
    for nb in (0, 1, n_blocks - 1):
        start = int(np.clip((nb - 1) * SWA_BLOCK, 0, n_lat - SWA_BAND))
        delta = nb * SWA_BLOCK + r - (start + j)
        tabs.append(np.where(np.abs(delta) <= SWA_WINDOW, 0.0, NEG_INF))
    tabs.append(np.full((SWA_BLOCK, SWA_BAND), NEG_INF))
    return jnp.asarray(np.stack(tabs), F32)


def _swa_kernel(sink_ref, q_ref, k_ref, v_ref, kc_ref, vc_ref, mask_ref, o_ref, s_ref, p_ref, den_ref, *, n_lat, group):
    kvh = pl.program_id(1)
    nb = pl.program_id(2)
    start = pl.multiple_of(jnp.clip((nb - 1) * SWA_BLOCK, 0, n_lat - SWA_BAND), SWA_BLOCK)
    kl = k_ref[pl.ds(start, SWA_BAND), :]
    vl = v_ref[pl.ds(start, SWA_BAND), :]
    q = q_ref[...]
    lane = lax.broadcasted_iota(jnp.int32, (1, LANES), 1)
    first_head = (lane & (SWA_HEAD_DIM - 1)) < SWA_HEAD_DIM // 2
    scale = SWA_HEAD_DIM ** -0.5
    parts = []
    for g in range(group):
        qp = q[:, (g // 2) * LANES:(g // 2 + 1) * LANES]
        sel = first_head if g % 2 == 0 else jnp.logical_not(first_head)
        parts.append(jnp.where(sel, qp, jnp.zeros_like(qp)) * scale)
    qs = jnp.concatenate(parts, axis=0)
    s_ref[:, :SWA_BAND] = _dot_nt(qs, kl)
    s_ref[:, SWA_BAND:] = _dot_nt(qs, kc_ref[...])

    def softmax_rows(i, carry):
        for u in range(ATT_ROWS_UNROLL):
            r0 = pl.multiple_of((i * ATT_ROWS_UNROLL + u) * ATT_ROWS, ATT_ROWS)
            rows = pl.ds(r0, ATT_ROWS)
            mrows = pl.ds(pl.multiple_of(r0 & (SWA_BLOCK - 1), ATT_ROWS), ATT_ROWS)
            s_lat = s_ref[rows, :SWA_BAND] + mask_ref[0, mrows, :]
            s_ctx = s_ref[rows, SWA_BAND:]
            blocks = ([s_lat[:, c * LANES:(c + 1) * LANES] for c in range(SWA_BAND // LANES)]
                      + [s_ctx[:, c * LANES:(c + 1) * LANES] for c in range(s_ctx.shape[1] // LANES)])
            sink = jnp.full((ATT_ROWS, 1), sink_ref[kvh * group + r0 // SWA_BLOCK], F32)
            m = jnp.maximum(jnp.max(functools.reduce(jnp.maximum, blocks), axis=1, keepdims=True), sink)
            es = [jnp.exp(b - m) for b in blocks]
            den = jnp.sum(functools.reduce(jnp.add, es), axis=1, keepdims=True) + jnp.exp(sink - m)
            for c, e in enumerate(es):
                p_ref[rows, c * LANES:(c + 1) * LANES] = e.astype(BF16)
            den_ref[rows, :] = jnp.broadcast_to(den, (ATT_ROWS, LANES))
        return carry

    lax.fori_loop(0, group * SWA_BLOCK // (ATT_ROWS * ATT_ROWS_UNROLL), softmax_rows, 0)
    o = (_dot(p_ref[:, :SWA_BAND], vl) + _dot(p_ref[:, SWA_BAND:], vc_ref[...])) / den_ref[...]
    for p in range(group // 2):
        o_a = o[(2 * p) * SWA_BLOCK:(2 * p + 1) * SWA_BLOCK]
        o_b = o[(2 * p + 1) * SWA_BLOCK:(2 * p + 2) * SWA_BLOCK]
        o_ref[:, p * LANES:(p + 1) * LANES] = jnp.where(lane < SWA_HEAD_DIM, o_a, o_b).astype(BF16)


def _window_attention(att, sink, n_batch, n_lat, n_ctx, with_ctx):
    T = att.shape[0]
    group = SWA_Q_HEADS // SWA_KV_HEADS
    qw = group * SWA_HEAD_DIM
    n_lat_blocks = n_lat // SWA_BLOCK
    n_ctx_blocks = n_ctx // SWA_BLOCK if with_ctx else 0
    k_col = SWA_Q_HEADS * SWA_HEAD_DIM // LANES
    v_col = k_col + SWA_KV_HEADS
    ctx_row = n_batch * n_lat // n_ctx

    def q_map(b, kvh, nb):
        lat = b * n_lat_blocks + nb
        ctx = n_batch * n_lat_blocks + b * (n_ctx // SWA_BLOCK) + (nb - n_lat_blocks)
        return (jnp.where(nb < n_lat_blocks, lat, ctx), kvh)

    def cls_map(b, kvh, nb):
        cls = jnp.where(nb == 0, 0, jnp.where(nb == n_lat_blocks - 1, 2, jnp.where(nb >= n_lat_blocks, 3, 1)))
        return (cls, 0, 0)

    grid_spec = pltpu.PrefetchScalarGridSpec(
        num_scalar_prefetch=0,
        grid=(n_batch, SWA_KV_HEADS, n_lat_blocks + n_ctx_blocks),
        in_specs=[pl.BlockSpec(memory_space=pltpu.SMEM),
                  pl.BlockSpec((SWA_BLOCK, qw), q_map),
                  pl.BlockSpec((n_lat, LANES), lambda b, kvh, nb: (b, k_col + kvh)),
                  pl.BlockSpec((n_lat, LANES), lambda b, kvh, nb: (b, v_col + kvh)),
                  pl.BlockSpec((n_ctx, LANES), lambda b, kvh, nb: (ctx_row + b, k_col + kvh)),
                  pl.BlockSpec((n_ctx, LANES), lambda b, kvh, nb: (ctx_row + b, v_col + kvh)),
                  pl.BlockSpec((1, SWA_BLOCK, SWA_BAND), cls_map)],
        out_specs=pl.BlockSpec((SWA_BLOCK, qw), q_map),
        scratch_shapes=[pltpu.VMEM((group * SWA_BLOCK, SWA_BAND + n_ctx), F32),
                        pltpu.VMEM((group * SWA_BLOCK, SWA_BAND + n_ctx), BF16),
                        pltpu.VMEM((group * SWA_BLOCK, LANES), F32)],
    )
    return pl.pallas_call(
        functools.partial(_swa_kernel, n_lat=n_lat, group=group),
        grid_spec=grid_spec,
        out_shape=jax.ShapeDtypeStruct((T if with_ctx else n_batch * n_lat, SWA_Q_HEADS * SWA_HEAD_DIM), BF16),
        compiler_params=_params(("arbitrary", "arbitrary", "arbitrary")), name="window_attention",
    )(sink.astype(F32), att, att, att, att, att, _swa_mask_table(n_lat))


DN_PAIR = 2 * DN_CHUNK
DN_GROUP = 8
DN_NQ = 2 * (DN_HEAD_DIM + DN_CHUNK)


def _bdot(a, b):
    return jnp.einsum('gik,gkj->gij', a, b, preferred_element_type=F32)


def _bdot_nt(a, b):
    return jnp.einsum('gik,gjk->gij', a, b, preferred_element_type=F32)


def _split2(x):
    hi = x.astype(BF16)
    return hi, (x - hi.astype(F32)).astype(BF16)


def _softplus(x):
    return jnp.maximum(x, 0.0) + jnp.log(1.0 + jnp.exp(-jnp.abs(x)))


def _pack_dot(a0, b0, a1, b1):
    z = jnp.zeros_like(b0)
    rhs = jnp.concatenate([jnp.concatenate([b0, z], axis=2), jnp.concatenate([z, b1], axis=2)], axis=1)
    r = _bdot(jnp.concatenate([a0, a1], axis=2), rhs)
    return r[:, :, :DN_PAIR], r[:, :, DN_PAIR:]


def _pack_dot3(a0, b0, a1, b1):
    (a0h, a0l), (b0h, b0l), (a1h, a1l), (b1h, b1l) = _split2(a0), _split2(b0), _split2(a1), _split2(b1)
    hh = _pack_dot(a0h, b0h, a1h, b1h)
    hl = _pack_dot(a0h, b0l, a1h, b1l)
    lh = _pack_dot(a0l, b0h, a1l, b1h)
    return hh[0] + (hl[0] + lh[0]), hh[1] + (hl[1] + lh[1])


def _dn_prepare(q, k, v, gcs, betas):
    G = q.shape[0]
    ii = lax.broadcasted_iota(jnp.int32, (DN_PAIR, DN_PAIR), 0)
    jj = lax.broadcasted_iota(jnp.int32, (DN_PAIR, DN_PAIR), 1)
    same = (ii >> 6) == (jj >> 6)
    incl = (same & (ii >= jj), same & (ii <= jj))
    strict = (same & (ii > jj), same & (ii < jj))
    eye = (ii == jj).astype(F32)
    eye16 = jnp.broadcast_to(eye.astype(BF16), (G, DN_PAIR, DN_PAIR))
    lane0 = jnp.broadcast_to((jj == 0).astype(BF16), (G, DN_PAIR, DN_PAIR))
    g_rows = sum(_bdot_nt(lane0, jnp.concatenate([pf, pb], axis=1)) for pf, pb in zip(_split3(gcs[0]), _split3(gcs[1])))
    g_row = (g_rows[:, :, :DN_PAIR], g_rows[:, :, DN_PAIR:])
    decay = [jnp.where(incl[d], jnp.exp(jnp.where(incl[d], gcs[d] - g_row[d], 0.0)), 0.0) for d in range(2)]
    k16 = k.astype(BF16)
    k_t = _bdot_nt(eye16, k16).astype(BF16)
    kb = [k * betas[d] for d in range(2)]
    vb = [v * betas[d] for d in range(2)]
    kk = _pack_dot(kb[0].astype(BF16), k_t, kb[1].astype(BF16), k_t)
    m = [jnp.where(strict[d], kk[d] * decay[d], 0.0) for d in range(2)]
    inv = [eye - m[0], eye - m[1]]
    p = m
    for _ in range(5):
        p = _pack_dot3(p[0], p[0], p[1], p[1])
        upd = _pack_dot3(inv[0], p[0], inv[1], p[1])
        inv = [inv[0] + upd[0], inv[1] + upd[1]]
    qk = _bdot(q.astype(BF16), k_t)
    first_chunk = lax.broadcasted_iota(jnp.int32, (DN_PAIR, 1), 0) < DN_CHUNK
    uws, a_intras, qgs, kgs, egls = [], [], [], [], []
    for d in range(2):
        gc = gcs[d]
        eg = jnp.exp(gc)
        uws.append(_bdot(inv[d].astype(BF16), jnp.concatenate([kb[d] * eg, vb[d]], axis=2).astype(BF16)).astype(BF16))
        a_intras.append(jnp.where(incl[d], qk * decay[d], 0.0).astype(BF16))
        lo, hi = (DN_CHUNK - 1, DN_PAIR - 1) if d == 0 else (0, DN_CHUNK)
        g_last = jnp.where(first_chunk, gc[:, lo:lo + 1, :], gc[:, hi:hi + 1, :])
        kgs.append((k * jnp.exp(g_last - gc)).astype(BF16))
        qgs.append(q * eg)
        egls.append(jnp.exp(g_last))
    kg_ts = _bdot_nt(eye16, jnp.concatenate(kgs, axis=1)).astype(BF16)
    zero = jnp.zeros((G, DN_PAIR, 2 * LANES), BF16)
    res = []
    for d in range(2):
        wu = uws[d]
        au = _bdot(a_intras[d], wu)
        q_eff = (qgs[d] - au[:, :, :LANES]).astype(BF16)
        wu_halves = jnp.concatenate([jnp.where(first_chunk, wu, zero), jnp.where(first_chunk, zero, wu)], axis=2)
        nb = _bdot(kg_ts[:, :, d * DN_PAIR:(d + 1) * DN_PAIR], wu_halves)
        nq = jnp.concatenate([(-nb[:, :, :LANES]).astype(BF16), q_eff[:, :DN_CHUNK],
                              (-nb[:, :, 2 * LANES:3 * LANES]).astype(BF16), q_eff[:, DN_CHUNK:]], axis=1)
        bb = jnp.concatenate([nb[:, :, LANES:2 * LANES], nb[:, :, 3 * LANES:]], axis=1)
        res.append((nq, bb, au[:, :, LANES:], egls[d]))
    return res


def _dn_sequence(q_ref, k_ref, v_ref, z_ref, ab_ref, cw_refs, gpar_ref, nw_ref, o_ref, s0, head, T, sc):
    (qn, kn, vn, gcs, bts, nqs, bss, egls, outs) = sc
    row = lax.broadcasted_iota(jnp.int32, (T, 1), 0)

    def conv_silu(x, w_ref):
        acc = x * w_ref[DN_CONV // 2:DN_CONV // 2 + 1, :]
        for j in range(DN_CONV):
            d = j - DN_CONV // 2
            if d == 0:
                continue
            shifted = pltpu.roll(x, (-d) % T, axis=0)
            ok = (row + d >= 0) & (row + d < T)
            acc = acc + jnp.where(ok, shifted, 0.0) * w_ref[j:j + 1, :]
        return acc * _sigmoid(acc)

    q = conv_silu(q_ref[...], cw_refs[0])
    k = conv_silu(k_ref[...], cw_refs[1])
    v = conv_silu(v_ref[...], cw_refs[2])
    q = q * lax.rsqrt(jnp.sum(q * q, axis=1, keepdims=True) + RMS_EPS) * DN_HEAD_DIM ** -0.5
    k = k * lax.rsqrt(jnp.sum(k * k, axis=1, keepdims=True) + RMS_EPS)
    qn[pl.ds(0, T), :] = q
    kn[pl.ds(0, T), :] = k
    vn[pl.ds(0, T), :] = v

    ab = ab_ref[...]
    g_all = -jnp.exp(gpar_ref[0:1, :]) * _softplus(ab + gpar_ref[1:2, :])
    b_all = _sigmoid(ab)
    lane = lax.broadcasted_iota(jnp.int32, (1, LANES), 1)
    src = jnp.where(lane < 2 * DN_HEADS, g_all, b_all)
    krow = lax.broadcasted_iota(jnp.int32, (3 * LANES, 4 * LANES), 0) & (LANES - 1)
    kcol = lax.broadcasted_iota(jnp.int32, (3 * LANES, 4 * LANES), 1) >> 7
    pick = (krow == kcol * DN_HEADS + head).astype(BF16)
    cols = _dot(jnp.concatenate(_split3(src), axis=1), pick)

    pos = row & (DN_CHUNK - 1)
    for d in range(2):
        g = cols[:, d * LANES:(d + 1) * LANES]
        for sh in (1, 2, 4, 8, 16, 32):
            if d == 0:
                g = g + jnp.where(pos >= sh, pltpu.roll(g, sh, axis=0), 0.0)
            else:
                g = g + jnp.where(pos < DN_CHUNK - sh, pltpu.roll(g, T - sh, axis=0), 0.0)
        gcs[d][pl.ds(0, T), :] = g
        bts[d][pl.ds(0, T), :] = cols[:, (2 + d) * LANES:(3 + d) * LANES]

    n_pairs = T // DN_PAIR
    G = min(DN_GROUP, n_pairs)
    R = G * DN_PAIR

    def prepare(gi, carry):
        rows = pl.ds(pl.multiple_of(gi * R, R), R)
        shape = (G, DN_PAIR, LANES)
        qq = qn[rows, :].reshape(shape)
        kk = kn[rows, :].reshape(shape)
        vv = vn[rows, :].reshape(shape)
        res = _dn_prepare(qq, kk, vv, [gcs[d][rows, :].reshape(shape) for d in range(2)],
                          [bts[d][rows, :].reshape(shape) for d in range(2)])
        for d in range(2):
            nq, bb, o_const, egl = res[d]
            nqs[d][pl.ds(pl.multiple_of(gi * G * DN_NQ, G * DN_NQ), G * DN_NQ), :] = nq.reshape(G * DN_NQ, LANES)
            bss[d][pl.ds(pl.multiple_of(gi * 2 * R, 2 * R), 2 * R), :] = bb.reshape(2 * R, LANES)
            outs[d][rows, :] = o_const.reshape(R, LANES)
            egls[d][rows, :] = egl.reshape(R, LANES)
        return carry

    lax.fori_loop(0, n_pairs // G, prepare, 0)

    def chunk_step(S, pair, half, d):
        off = pl.multiple_of(pair * DN_PAIR + half * DN_CHUNK, DN_CHUNK)
        rows = pl.ds(off, DN_CHUNK)
        lhs = nqs[d][pl.ds(pl.multiple_of(pair * DN_NQ + half * (DN_NQ // 2), DN_NQ // 2), DN_NQ // 2), :]
        r = _dot(lhs, S.astype(BF16))
        outs[d][rows, :] = outs[d][rows, :] + r[DN_HEAD_DIM:]
        decay = jnp.broadcast_to(egls[d][pl.ds(off, 1), :], (DN_HEAD_DIM, LANES))
        add = bss[d][pl.ds(pl.multiple_of(pair * 2 * DN_HEAD_DIM + half * DN_HEAD_DIM, DN_HEAD_DIM), DN_HEAD_DIM), :]
        return S * decay + (r[:DN_HEAD_DIM] + add)

    def scan(p, carry):
        s_f, s_b = carry
        pb = n_pairs - 1 - p
        s_f = chunk_step(s_f, p, 0, 0)
        s_b = chunk_step(s_b, pb, 1, 1)
        s_f = chunk_step(s_f, p, 1, 0)
        s_b = chunk_step(s_b, pb, 0, 1)
        return s_f, s_b

    s_f, s_b = lax.fori_loop(0, n_pairs, scan, s0)

    o = outs[0][pl.ds(0, T), :] + outs[1][pl.ds(0, T), :]
    o = o * lax.rsqrt(jnp.mean(o * o, axis=1, keepdims=True) + RMS_EPS) * nw_ref[...]
    z = z_ref[...]
    o_ref[...] = (o * (z * _sigmoid(z))).astype(BF16)
    return s_f, s_b


def _dn_kernel(ql, kl, vl, zl, abl, qc, kc, vc, zc, abc, cwq, cwk, cwv, gpar, nw, ol, oc, *scratch, n_lat, n_ctx):
    head = pl.program_id(1)
    it = iter(scratch)
    sc = [next(it) if i < 3 else (next(it), next(it)) for i in range(9)]
    zero = jnp.zeros((DN_HEAD_DIM, DN_HEAD_DIM), F32)
    s_ctx = _dn_sequence(qc, kc, vc, zc, abc, (cwq, cwk, cwv), gpar, nw, oc, (zero, zero), head, n_ctx, sc)
    _dn_sequence(ql, kl, vl, zl, abl, (cwq, cwk, cwv), gpar, nw, ol, s_ctx, head, n_lat, sc)


def _delta_net(dn, conv_w, a_log, dt_bias, norm_w, n_batch, n_lat, n_ctx):
    H = DN_HEADS
    ctx_row = n_batch * n_lat // n_ctx
    ab_col = 4 * H

    def lat(off):
        return pl.BlockSpec((n_lat, LANES), lambda b, h: (b, off + h))

    def ctx(off):
        return pl.BlockSpec((n_ctx, LANES), lambda b, h: (ctx_row + b, off + h))

    cw = jnp.zeros((SUBLANES, 3 * DN_WIDTH), F32).at[:DN_CONV].set(conv_w.astype(F32))
    gpar = jnp.zeros((SUBLANES, LANES), F32)
    gpar = gpar.at[0, :2 * H].set(a_log.reshape(-1).astype(F32)).at[1, :2 * H].set(dt_bias.reshape(-1).astype(F32))
    in_specs = [lat(0), lat(H), lat(2 * H), lat(3 * H), pl.BlockSpec((n_lat, LANES), lambda b, h: (b, ab_col)),
                ctx(0), ctx(H), ctx(2 * H), ctx(3 * H), pl.BlockSpec((n_ctx, LANES), lambda b, h: (ctx_row + b, ab_col)),
                pl.BlockSpec((SUBLANES, LANES), lambda b, h: (0, h)),
                pl.BlockSpec((SUBLANES, LANES), lambda b, h: (0, H + h)),
                pl.BlockSpec((SUBLANES, LANES), lambda b, h: (0, 2 * H + h)),
                pl.BlockSpec((SUBLANES, LANES), lambda b, h: (0, 0)),
                pl.BlockSpec((1, LANES), lambda b, h: (0, 0))]
    f32s = lambda rows=n_lat: pltpu.VMEM((rows, LANES), F32)
    n_pairs = n_lat // DN_PAIR
    scratch = [f32s(), f32s(), f32s()]
    scratch += [f32s() for _ in range(4)]
    scratch += [pltpu.VMEM((n_pairs * DN_NQ, LANES), BF16) for _ in range(2)]
    scratch += [f32s(2 * n_pairs * DN_HEAD_DIM) for _ in range(2)]
    scratch += [f32s() for _ in range(4)]
    return pl.pallas_call(
        functools.partial(_dn_kernel, n_lat=n_lat, n_ctx=n_ctx),
        grid=(n_batch, H),
        in_specs=in_specs,
        out_specs=[pl.BlockSpec((n_lat, LANES), lambda b, h: (b, h)),
                   pl.BlockSpec((n_ctx, LANES), lambda b, h: (b, h))],
        out_shape=[jax.ShapeDtypeStruct((n_batch * n_lat, DN_WIDTH), BF16),
                   jax.ShapeDtypeStruct((n_batch * n_ctx, DN_WIDTH), BF16)],
        scratch_shapes=scratch,
        compiler_params=_params(("arbitrary", "arbitrary")), name="delta_net",
    )(dn, dn, dn, dn, dn, dn, dn, dn, dn, dn, cw, cw, cw, gpar, norm_w.reshape(1, -1).astype(F32))


def _rope_tables(n_tokens):
    t = jnp.arange(n_tokens)
    n_freq = SWA_HEAD_DIM // 4
    inv = ROPE_THETA ** (-jnp.arange(n_freq, dtype=F32) / n_freq)
    row = (t // GRID_W).astype(F32)[:, None]
    col = (t % GRID_W).astype(F32)[:, None]
    ang = jnp.concatenate([row * inv, col * inv], -1)
    cos, sin = jnp.cos(ang), jnp.sin(ang)
    cos_t = jnp.concatenate([cos, cos, cos, cos], -1)
    sin_t = jnp.concatenate([-sin, -sin, sin, sin], -1)
    ident = jnp.ones((TOK_TILE, LANES), F32)
    return (jnp.concatenate([cos_t, ident], 0), jnp.concatenate([sin_t, jnp.zeros_like(ident)], 0))


def _swa_weight(w_in):
    D = w_in.shape[0]
    qw = SWA_Q_HEADS * SWA_HEAD_DIM
    kw = SWA_KV_HEADS * SWA_HEAD_DIM
    half = SWA_HEAD_DIM // 2
    wq = w_in[:, :qw].reshape(D, SWA_Q_HEADS // 2, 2, 2, half)
    wq = jnp.transpose(wq, (0, 1, 3, 2, 4)).reshape(D, qw)
    wk = w_in[:, qw:qw + kw].reshape(D, SWA_KV_HEADS, 2, 1, half)
    wk = jnp.broadcast_to(wk, (D, SWA_KV_HEADS, 2, 2, half)).reshape(D, 2 * kw)
    wv = w_in[:, qw + kw:].reshape(D, SWA_KV_HEADS, 1, SWA_HEAD_DIM)
    wv = jnp.broadcast_to(wv, (D, SWA_KV_HEADS, 2, SWA_HEAD_DIM)).reshape(D, 2 * kw)
    return jnp.concatenate([wq, wk, wv], axis=1).astype(BF16)


def kernel(x, c, ctx, c_ctx, w_mod, b_mod, ln_g, ln_b, w_in_ab, na_rpb, dn_conv, dn_a_log, dn_dt_bias, dn_norm_w,
           w_out_ab, w_in_c, swa_sink, w_out_c, w_router, b_router, w_gu, b_gu, w_down, b_down):
    B, N, D = x.shape
    L = ctx.shape[1]
    depth = w_mod.shape[0]
    alpha = (2 * depth) ** 0.25
    assert D == D_MODEL and N % TOK_TILE == 0 and L == TOK_TILE
    geom = (B * N // TOK_TILE, N // TOK_TILE, B)

    mod_rows = -(-(B + 1) // 16) * 16
    cc = jnp.zeros((mod_rows, D), F32).at[:B].set(c).at[B].set(c_ctx)
    mod_all = _modulation(cc, w_mod, b_mod)
    cos_t, sin_t = _rope_tables(N)
    n_att_ab = 3 * NA_WIDTH
    ab_main = n_att_ab + 4 * DN_WIDTH
    wr = jnp.zeros((depth, D, LANES), BF16).at[:, :, :N_EXPERTS].set(w_router.astype(BF16))
    br = jnp.full((depth, 1, LANES), NEG_INF, F32).at[:, 0, :N_EXPERTS].set(b_router.astype(F32))
    wgu16, wd16 = w_gu.astype(BF16), w_down.astype(BF16)
    bgu4, bd4 = b_gu[:, :, None, :].astype(F32), b_down[:, :, None, :].astype(F32)

    h = jnp.concatenate([x.reshape(B * N, D), ctx.reshape(B * L, D)], axis=0)
    moe_prev = mod_prev = ln_prev = None
    out = None
    for layer in range(depth):
        last = layer == depth - 1
        i = layer // 2
        mod = mod_all[layer].reshape(mod_rows, 1, -1)
        if layer % 2 == 0:
            w = w_in_ab[i]
            w = jnp.concatenate([w[:, :ab_main], jnp.pad(w[:, ab_main:], ((0, 0), (0, LANES - (w.shape[1] - ab_main))))], 1)
            h, att, dn = _proj_in(h, moe_prev, mod_prev, ln_prev, mod, w.astype(BF16), geom, n_att=n_att_ab, alpha=alpha)
            o_a = _neighbourhood_attention(att, _na_bias_table(na_rpb[i], N // GRID_W), B, N, L)
            o_bl, o_bc = _delta_net(dn, dn_conv[i], dn_a_log[i], dn_dt_bias[i], dn_norm_w[i], B, N, L)
            xs = [o_a, jnp.concatenate([o_bl, o_bc], axis=0)]
            wo = w_out_ab[i].astype(BF16)
            ws = [wo[:NA_WIDTH], wo[NA_WIDTH:]]
        else:
            w = _swa_weight(w_in_c[i])
            n_rope = (SWA_Q_HEADS + 2 * SWA_KV_HEADS) * SWA_HEAD_DIM // LANES
            h, att, _ = _proj_in(h, moe_prev, mod_prev, ln_prev, mod, w, geom, n_att=w.shape[1],
                                 rope=(cos_t, sin_t, n_rope), alpha=alpha)
            xs = [_window_attention(att, swa_sink[i], B, N, L, with_ctx=not last)]
            ws = [w_out_c[i].astype(BF16)]
        n_rows = B * N if last else B * (N + L)
        ln1 = jnp.stack([ln_g[layer, 0], ln_b[layer, 0]])
        ln2 = jnp.stack([ln_g[layer, 1], ln_b[layer, 1]])
        h1, f_moe, idx, gates = _proj_out(xs, ws, h, mod, ln1, wr[layer], br[layer], geom, n_rows, alpha=alpha)
        y_moe = _moe(f_moe, idx, gates, wgu16, bgu4, wd16, bd4, layer)
        if last:
            out = _final_combine(h1, y_moe, mod, ln2, geom, alpha=alpha).reshape(B, N, D)
        else:
            h, moe_prev, mod_prev, ln_prev = h1, y_moe, mod, ln2
    return out
```
